```python
import math
import jax, jax.numpy as jnp
from jax import lax
import numpy as np

D_MODEL = 1024
BATCH = 16
SEQ = 4096
DEPTH = 2
DEC_BATCH = 8
DEC_SEQ = 16
PAST_LEN = 4096

CHUNK = 64
D_MIX = D_MODEL
D_LRU = D_MIX // 4
LRU_HEADS = 4
LRU_BLOCK = D_LRU // LRU_HEADS
CONV_W = 4
LRU_C = 8.0
D_SSM = D_MIX // 4
SSM_GROUP = 16
SSM_GROUPS = D_SSM // SSM_GROUP
SSM_STATE = 64
HEAD_DIM = 64
N_Q_HEADS = (D_MIX // 2) // HEAD_DIM
N_KV_HEADS = 2
GQA_GROUP = N_Q_HEADS // N_KV_HEADS
D_ATT = N_Q_HEADS * HEAD_DIM
D_KV = N_KV_HEADS * HEAD_DIM
WINDOW = 128
BAND_PREV = WINDOW // CHUNK
ROPE_THETA = 10000.0
D_IN = 2 * D_LRU + D_SSM + D_ATT + 2 * D_KV
D_FF = 4 * D_MODEL
RMS_EPS = 1e-6
NEG_INF = -1e30

kernel_name = "hybrid_rglru_s5_swa_stream_step"

F32 = jnp.float32


def _rms_norm(x, g):
    xf = x.astype(F32)
    y = xf * lax.rsqrt(jnp.mean(xf * xf, axis=-1, keepdims=True) + RMS_EPS)
    return (y * g.astype(F32)).astype(x.dtype)


def _rope(x, pos):
    half = HEAD_DIM // 2
    inv = ROPE_THETA ** (-jnp.arange(half, dtype=F32) / half)
    ang = pos.astype(F32)[:, None] * inv[None, :]
    cos = jnp.cos(ang)[None, :, None, :]
    sin = jnp.sin(ang)[None, :, None, :]
    xf = x.astype(F32)
    x1, x2 = xf[..., :half], xf[..., half:]
    return jnp.concatenate([x1 * cos - x2 * sin, x2 * cos + x1 * sin], axis=-1).astype(x.dtype)


def _lin_combine(e1, e2):
    a1, b1 = e1
    a2, b2 = e2
    return a1 * a2, a2 * b1 + b2


def _cplx_combine(e1, e2):
    a1r, a1i, b1r, b1i = e1
    a2r, a2i, b2r, b2i = e2
    return (a2r * a1r - a2i * a1i,
            a2r * a1i + a2i * a1r,
            a2r * b1r - a2i * b1i + b2r,
            a2r * b1i + a2i * b1r + b2i)


def _rglru_mixer(u, gate, conv_buf, h0, conv_w, conv_b, w_r, b_r, w_i, b_i, lam):
    B, L = u.shape[0], u.shape[1]
    xx = jnp.concatenate([conv_buf.astype(u.dtype), u], axis=1)
    xc = conv_b.astype(F32)
    for j in range(CONV_W):
        xc = xc + xx[:, j:j + L].astype(F32) * conv_w[j].astype(F32)
    new_buf = xx[:, L:]
    xb = xc.reshape(B, L, LRU_HEADS, LRU_BLOCK)
    r = jax.nn.sigmoid(jnp.einsum('blhi,hij->blhj', xb, w_r.astype(F32)).reshape(B, L, D_LRU) + b_r.astype(F32))
    i = jax.nn.sigmoid(jnp.einsum('blhi,hij->blhj', xb, w_i.astype(F32)).reshape(B, L, D_LRU) + b_i.astype(F32))
    log_a = -LRU_C * r * jax.nn.softplus(-lam.astype(F32))
    a = jnp.exp(log_a)
    b = jnp.sqrt(-jnp.expm1(2.0 * log_a)) * (i * xc)
    b = b.at[:, 0].add(a[:, 0] * h0.astype(F32))
    h = lax.associative_scan(_lin_combine, (a, b), axis=1)[1]
    out = h * jax.nn.gelu(gate.astype(F32))
    return out.astype(u.dtype), new_buf, h[:, -1].astype(u.dtype)


def _s5_mixer(u, s0_re, s0_im, a_re, a_im, b_re, b_im, c_re, c_im, d, log_dt, w_glu, b_glu):
    B, L = u.shape[0], u.shape[1]
    dt = jnp.exp(log_dt.astype(F32))[:, None]
    lr, li = a_re.astype(F32), a_im.astype(F32)
    mag = jnp.exp(lr * dt)
    abar_r, abar_i = mag * jnp.cos(li * dt), mag * jnp.sin(li * dt)
    den = lr * lr + li * li
    nr = abar_r - 1.0
    fr = (nr * lr + abar_i * li) / den
    fi = (abar_i * lr - nr * li) / den
    br, bi = b_re.astype(F32), b_im.astype(F32)
    bb_r = fr[..., None] * br - fi[..., None] * bi
    bb_i = fr[..., None] * bi + fi[..., None] * br
    uf = u.astype(F32)
    ug = uf.reshape(B, L, SSM_GROUPS, SSM_GROUP)
    bu_r = jnp.einsum('blgc,gpc->blgp', ug, bb_r)
    bu_i = jnp.einsum('blgc,gpc->blgp', ug, bb_i)
    sr0, si0 = s0_re.astype(F32), s0_im.astype(F32)
    bu_r = bu_r.at[:, 0].add(abar_r * sr0 - abar_i * si0)
    bu_i = bu_i.at[:, 0].add(abar_r * si0 + abar_i * sr0)
    ar = jnp.broadcast_to(abar_r, bu_r.shape)
    ai = jnp.broadcast_to(abar_i, bu_i.shape)
    _, _, s_r, s_i = lax.associative_scan(_cplx_combine, (ar, ai, bu_r, bu_i), axis=1)
    y = (jnp.einsum('blgp,gcp->blgc', s_r, c_re.astype(F32))
         - jnp.einsum('blgp,gcp->blgc', s_i, c_im.astype(F32))).reshape(B, L, D_SSM)
    y = y + d.astype(F32) * uf
    z = jax.nn.gelu(y)
    out = z * jax.nn.sigmoid(z @ w_glu.astype(F32) + b_glu.astype(F32))
    return out.astype(u.dtype), s_r[:, -1].astype(u.dtype), s_i[:, -1].astype(u.dtype)


def _sink_softmax(scores, sinks, mask=None):
    s = sinks.astype(F32).reshape(N_KV_HEADS, GQA_GROUP, 1, 1)
    if mask is not None:
        scores = jnp.where(mask, scores, NEG_INF)
    m = jnp.maximum(jnp.max(scores, axis=-1, keepdims=True), s)
    e = jnp.exp(scores - m)
    return e / (jnp.sum(e, axis=-1, keepdims=True) + jnp.exp(s - m))


def _swa_prompt(q, k, v, sinks):
    B, L = q.shape[0], q.shape[1]
    nc = L // CHUNK
    scale = HEAD_DIM ** -0.5
    qc = q.astype(F32).reshape(B, nc, CHUNK, N_KV_HEADS, GQA_GROUP, HEAD_DIM)
    kc = k.astype(F32).reshape(B, nc, CHUNK, N_KV_HEADS, HEAD_DIM)
    vc = v.astype(F32).reshape(B, nc, CHUNK, N_KV_HEADS, HEAD_DIM)
    pad = ((0, 0), (BAND_PREV, 0), (0, 0), (0, 0), (0, 0))
    kp, vp = jnp.pad(kc, pad), jnp.pad(vc, pad)
    kb = jnp.concatenate([kp[:, j:j + nc] for j in range(BAND_PREV + 1)], axis=2)
    vb = jnp.concatenate([vp[:, j:j + nc] for j in range(BAND_PREV + 1)], axis=2)
    scores = jnp.einsum('bnqkgd,bnskd->bnkgqs', qc, kb) * scale
    key_chunk = jnp.arange(nc)[:, None] + (jnp.arange((BAND_PREV + 1) * CHUNK) // CHUNK)[None, :] - BAND_PREV
    mask = (key_chunk >= 0)[None, :, None, None, None, :]
    probs = _sink_softmax(scores, sinks, mask)
    out = jnp.einsum('bnkgqs,bnskd->bnqkgd', probs, vb)
    return out.reshape(B, L, D_ATT)


def _swa_cached(q, k, v, cache_k, cache_v, sinks):
    Bd, S = q.shape[0], q.shape[1]
    scale = HEAD_DIM ** -0.5
    kk = jnp.concatenate([cache_k.astype(F32), k.astype(F32)], axis=1)
    vv = jnp.concatenate([cache_v.astype(F32), v.astype(F32)], axis=1)
    qs = q.astype(F32).reshape(Bd, S, N_KV_HEADS, GQA_GROUP, HEAD_DIM)
    scores = jnp.einsum('bqkgd,bskd->bkgqs', qs, kk) * scale
    probs = _sink_softmax(scores, sinks)
    out = jnp.einsum('bkgqs,bskd->bqkgd', probs, vv)
    return out.reshape(Bd, S, D_ATT)


def _hybrid_layer(x, pos, conv_buf, h0, s0_re, s0_im, cache_k, cache_v, p):
    B, L = x.shape[0], x.shape[1]
    hn = _rms_norm(x, p['norm1'])
    proj = hn @ p['w_in']
    o1 = D_LRU
    o2 = o1 + D_LRU
    o3 = o2 + D_SSM
    o4 = o3 + D_ATT
    o5 = o4 + D_KV
    u_a, g_a, u_b = proj[..., :o1], proj[..., o1:o2], proj[..., o2:o3]
    q = proj[..., o3:o4].reshape(B, L, N_Q_HEADS, HEAD_DIM)
    k = proj[..., o4:o5].reshape(B, L, N_KV_HEADS, HEAD_DIM)
    v = proj[..., o5:].reshape(B, L, N_KV_HEADS, HEAD_DIM)

    out_a, conv_new, h_last = _rglru_mixer(u_a, g_a, conv_buf, h0, p['conv_w'], p['conv_b'],
                                           p['w_rg'], p['b_rg'], p['w_ig'], p['b_ig'], p['lru_lambda'])
    out_b, s_re, s_im = _s5_mixer(u_b, s0_re, s0_im, p['ssm_a_re'], p['ssm_a_im'], p['ssm_b_re'],
                                  p['ssm_b_im'], p['ssm_c_re'], p['ssm_c_im'], p['ssm_d'],
                                  p['ssm_log_dt'], p['w_glu'], p['b_glu'])
    q = _rope(q, pos)
    k = _rope(k, pos)
    if cache_k is None:
        out_c = _swa_prompt(q, k, v, p['attn_sinks'])
        k_new, v_new = k[:, -WINDOW:], v[:, -WINDOW:]
    else:
        out_c = _swa_cached(q, k, v, cache_k, cache_v, p['attn_sinks'])
        k_new, v_new = k, v

    mix = jnp.concatenate([out_a, out_b, out_c.astype(x.dtype)], axis=-1) @ p['w_out']
    x = x + mix
    hm = _rms_norm(x, p['norm2'])
    act = jnp.square(jax.nn.relu(hm @ p['w_up']))
    x = x + act @ p['w_down']
    return x, (conv_new, h_last, s_re, s_im, k_new, v_new)


def setup_inputs(seed: int = 0) -> dict:
    key = jax.random.key(seed)
    ks = jax.random.split(key, 40)
    nrm = lambda k, shape, s: jax.random.normal(k, shape, F32) * s
    a0 = jax.random.uniform(ks[10], (DEPTH, D_LRU), F32, 0.9, 0.999) ** (1.0 / LRU_C)
    lru_lambda = jnp.log(a0 / (1.0 - a0))
    ssm_a_re = -0.5 * jnp.exp(nrm(ks[11], (DEPTH, SSM_GROUPS, SSM_STATE), 0.05))
    ssm_a_im = math.pi * jnp.arange(SSM_STATE, dtype=F32)[None, None, :] + nrm(ks[12], (DEPTH, SSM_GROUPS, SSM_STATE), 0.01)
    ssm_log_dt = jax.random.uniform(ks[19], (DEPTH, SSM_GROUPS), F32, math.log(1e-3), math.log(1e-1))
    return {
        'x_prompt': nrm(ks[0], (BATCH, SEQ, D_MODEL), 1.0),
        'x_sample': nrm(ks[1], (DEC_BATCH, DEC_SEQ, D_MODEL), 1.0),
        'cache_conv_a': nrm(ks[2], (DEPTH, DEC_BATCH, CONV_W - 1, D_LRU), 1.0),
        'state_lru': nrm(ks[3], (DEPTH, DEC_BATCH, D_LRU), 0.5),
        'state_ssm_re': nrm(ks[4], (DEPTH, DEC_BATCH, SSM_GROUPS, SSM_STATE), 0.1),
        'state_ssm_im': nrm(ks[5], (DEPTH, DEC_BATCH, SSM_GROUPS, SSM_STATE), 0.1),
        'cache_k': nrm(ks[6], (DEPTH, DEC_BATCH, WINDOW, N_KV_HEADS, HEAD_DIM), 1.0),
        'cache_v': nrm(ks[7], (DEPTH, DEC_BATCH, WINDOW, N_KV_HEADS, HEAD_DIM), 1.0),
        'norm1': 1.0 + nrm(ks[8], (DEPTH, D_MODEL), 0.01),
        'w_in': nrm(ks[9], (DEPTH, D_MODEL, D_IN), D_MODEL ** -0.5),
        'conv_w': nrm(ks[13], (DEPTH, CONV_W, D_LRU), 0.5),
        'conv_b': nrm(ks[14], (DEPTH, D_LRU), 0.01),
        'w_rg': nrm(ks[15], (DEPTH, LRU_HEADS, LRU_BLOCK, LRU_BLOCK), LRU_BLOCK ** -0.5),
        'b_rg': nrm(ks[16], (DEPTH, D_LRU), 0.01),
        'w_ig': nrm(ks[17], (DEPTH, LRU_HEADS, LRU_BLOCK, LRU_BLOCK), LRU_BLOCK ** -0.5),
        'b_ig': nrm(ks[18], (DEPTH, D_LRU), 0.01),
        'lru_lambda': lru_lambda,
        'ssm_a_re': ssm_a_re,
        'ssm_a_im': ssm_a_im,
        'ssm_b_re': nrm(ks[20], (DEPTH, SSM_GROUPS, SSM_STATE, SSM_GROUP), (2 * SSM_GROUP) ** -0.5),
        'ssm_b_im': nrm(ks[21], (DEPTH, SSM_GROUPS, SSM_STATE, SSM_GROUP), (2 * SSM_GROUP) ** -0.5),
        'ssm_c_re': nrm(ks[22], (DEPTH, SSM_GROUPS, SSM_GROUP, SSM_STATE), SSM_STATE ** -0.5),
        'ssm_c_im': nrm(ks[23], (DEPTH, SSM_GROUPS, SSM_GROUP, SSM_STATE), SSM_STATE ** -0.5),
        'ssm_d': nrm(ks[24], (DEPTH, D_SSM), 0.5),
        'ssm_log_dt': ssm_log_dt,
        'w_glu': nrm(ks[25], (DEPTH, D_SSM, D_SSM), D_SSM ** -0.5),
        'b_glu': nrm(ks[26], (DEPTH, D_SSM), 0.01),
        'attn_sinks': nrm(ks[27], (DEPTH, N_Q_HEADS), 0.5),
        'w_out': nrm(ks[28], (DEPTH, D_MIX, D_MODEL), D_MIX ** -0.5),
        'norm2': 1.0 + nrm(ks[29], (DEPTH, D_MODEL), 0.01),
        'w_up': nrm(ks[30], (DEPTH, D_MODEL, D_FF), D_MODEL ** -0.5),
        'w_down': nrm(ks[31], (DEPTH, D_FF, D_MODEL), D_FF ** -0.5),
        'norm_f': 1.0 + nrm(ks[32], (D_MODEL,), 0.01),
    }


def reference(x_prompt, x_sample, cache_conv_a, state_lru, state_ssm_re, state_ssm_im, cache_k, cache_v,
              norm1, w_in, conv_w, conv_b, w_rg, b_rg, w_ig, b_ig, lru_lambda,
              ssm_a_re, ssm_a_im, ssm_b_re, ssm_b_im, ssm_c_re, ssm_c_im, ssm_d, ssm_log_dt,
              w_glu, b_glu, attn_sinks, w_out, norm2, w_up, w_down, norm_f):
    Bp, Lp = x_prompt.shape[0], x_prompt.shape[1]
    Ls = x_sample.shape[1]
    pos_p = jnp.arange(Lp, dtype=jnp.int32)
    pos_s = PAST_LEN + jnp.arange(Ls, dtype=jnp.int32)
    dt = x_prompt.dtype
    zero_conv = jnp.zeros((Bp, CONV_W - 1, D_LRU), dt)
    zero_h = jnp.zeros((Bp, D_LRU), dt)
    zero_s = jnp.zeros((Bp, SSM_GROUPS, SSM_STATE), dt)

    xp, xs = x_prompt, x_sample
    st_p = []
    st_s = []
    for l in range(DEPTH):
        p = dict(norm1=norm1[l], w_in=w_in[l], conv_w=conv_w[l], conv_b=conv_b[l], w_rg=w_rg[l],
                 b_rg=b_rg[l], w_ig=w_ig[l], b_ig=b_ig[l], lru_lambda=lru_lambda[l],
                 ssm_a_re=ssm_a_re[l], ssm_a_im=ssm_a_im[l], ssm_b_re=ssm_b_re[l], ssm_b_im=ssm_b_im[l],
                 ssm_c_re=ssm_c_re[l], ssm_c_im=ssm_c_im[l], ssm_d=ssm_d[l], ssm_log_dt=ssm_log_dt[l],
                 w_glu=w_glu[l], b_glu=b_glu[l], attn_sinks=attn_sinks[l], w_out=w_out[l],
                 norm2=norm2[l], w_up=w_up[l], w_down=w_down[l])
        xp, sp = _hybrid_layer(xp, pos_p, zero_conv, zero_h, zero_s, zero_s, None, None, p)
        xs, ss = _hybrid_layer(xs, pos_s, cache_conv_a[l], state_lru[l], state_ssm_re[l], state_ssm_im[l],
                               cache_k[l], cache_v[l], p)
        st_p.append(sp)
        st_s.append(ss)

    y_prompt = _rms_norm(xp, norm_f)
    y_sample = _rms_norm(xs, norm_f)
    conv_a_p = jnp.stack([s[0] for s in st_p], 0)
    lru_p = jnp.stack([s[1] for s in st_p], 0)
    ssm_re_p = jnp.stack([s[2] for s in st_p], 0)
    ssm_im_p = jnp.stack([s[3] for s in st_p], 0)
    k_p = jnp.stack([s[4] for s in st_p], 0)
    v_p = jnp.stack([s[5] for s in st_p], 0)
    conv_a_s = jnp.stack([s[0] for s in st_s], 0)
    lru_s = jnp.stack([s[1] for s in st_s], 0)
    ssm_re_s = jnp.stack([s[2] for s in st_s], 0)
    ssm_im_s = jnp.stack([s[3] for s in st_s], 0)
    k_s = jnp.stack([s[4] for s in st_s], 0)
    v_s = jnp.stack([s[5] for s in st_s], 0)
    return (y_prompt, y_sample, conv_a_p, lru_p, ssm_re_p, ssm_im_p, k_p, v_p,
            conv_a_s, lru_s, ssm_re_s, ssm_im_s, k_s, v_s)
```

```python
import functools
import math

import jax
import jax.numpy as jnp
from jax import lax
from jax.experimental import pallas as pl
from jax.experimental.pallas import tpu as pltpu

F32 = jnp.float32
BF16 = jnp.bfloat16

D_MODEL = 1024
PAST_LEN = 4096
CHUNK = 64
D_LRU = 256
LRU_HEADS = 4
CONV_W = 4
LRU_C = 8.0
D_SSM = 256
SSM_GROUP = 16
SSM_GROUPS = 16
SSM_STATE = 64
D_STATE = SSM_GROUPS * SSM_STATE
HEAD_DIM = 64
N_Q_HEADS = 8
N_KV_HEADS = 2
D_ATT = N_Q_HEADS * HEAD_DIM
D_KV = N_KV_HEADS * HEAD_DIM
WINDOW = 128
ROPE_THETA = 10000.0
D_AB = 2 * D_LRU + D_SSM
D_IN = D_AB + D_ATT + 2 * D_KV
D_FF = 4 * D_MODEL
RMS_EPS = 1e-6
NEG_INF = -1e30

LANES = 128
KEY_WIN = 2 * LANES
FF_CHUNK = 1024
SCAN_LANES = 256
VMEM_LIMIT = 56 * 1024 * 1024

T_ATTN = 256
T_SCAN = 64
T_MLP = 512


def _rms(x, g):
    return (x * lax.rsqrt(jnp.mean(x * x, axis=-1, keepdims=True) + RMS_EPS)) * g


def _const_spec(shape):
    nd = len(shape)
    return pl.BlockSpec(shape, lambda *_: (0,) * nd, pipeline_mode=pl.Buffered(1))


def _inproj_attn_kernel(pos_base, T, CS, x_ref, g1_ref, win_ref, cos_ref, sin_ref, ck_ref, cv_ref, sink_ref,
                        pab_ref, oc_ref, kout_ref, vout_ref, kext, vext):
    tc = pl.program_id(1)

    @pl.when(tc == 0)
    def _():
        kext[...] = jnp.zeros_like(kext)
        vext[...] = jnp.zeros_like(vext)
        kext[0:WINDOW, :] = ck_ref[0]
        vext[0:WINDOW, :] = cv_ref[0]

    hn = _rms(x_ref[0], g1_ref[...])
    proj = jnp.dot(hn.astype(BF16), win_ref[...], preferred_element_type=F32)
    pab_ref[...] = proj[:, :D_AB]

    cos = cos_ref[...]
    sin = sin_ref[...]
    lane = lax.broadcasted_iota(jnp.int32, (1, LANES), 1)
    first_half = (lane % HEAD_DIM) < (HEAD_DIM // 2)
    low_head = lane < HEAD_DIM

    def rope(a):
        partner = jnp.where(first_half, pltpu.roll(a, LANES - HEAD_DIM // 2, 1), pltpu.roll(a, HEAD_DIM // 2, 1))
        return a * cos + partner * sin

    scale = HEAD_DIM ** -0.5
    qcols = [(rope(proj[:, D_AB + j * LANES:D_AB + (j + 1) * LANES]) * scale).astype(BF16)
             for j in range(D_ATT // LANES)]
    k_rot = rope(proj[:, D_AB + D_ATT:D_AB + D_ATT + D_KV])
    v_new = proj[:, D_AB + D_ATT + D_KV:]
    kext[WINDOW:WINDOW + T, :] = k_rot
    vext[WINDOW:WINDOW + T, :] = v_new
    n_out = kout_ref.shape[1]
    kout_ref[0] = kext[WINDOW + T - n_out:WINDOW + T, :]
    vout_ref[0] = vext[WINDOW + T - n_out:WINDOW + T, :]

    def variants(full):
        swapped = pltpu.roll(full, HEAD_DIM, 1)
        zero = jnp.zeros_like(full)
        return ((jnp.where(low_head, full, zero).astype(BF16), jnp.where(low_head, zero, swapped).astype(BF16)),
                (jnp.where(low_head, swapped, zero).astype(BF16), jnp.where(low_head, zero, full).astype(BF16)))

    kvar = variants(kext[...])
    vvar = variants(vext[...])

    kidx = lax.broadcasted_iota(jnp.int32, (1, KEY_WIN), 1)
    row = lax.broadcasted_iota(jnp.int32, (2 * CS, 1), 0)
    for c in range(T // CS):
        lo = c * CS
        pos0 = pos_base + tc * T + lo
        valid = (kidx >= WINDOW - pos0) & (kidx < WINDOW + CS)
        for kv in range(N_KV_HEADS):
            q2 = jnp.concatenate([qcols[2 * kv][lo:lo + CS], qcols[2 * kv + 1][lo:lo + CS]], axis=0)
            out = None
            for half in range(2):
                kk = kvar[kv][half][lo:lo + KEY_WIN]
                vv = vvar[kv][half][lo:lo + KEY_WIN]
                s = lax.dot_general(q2, kk, (((1,), (1,)), ((), ())), preferred_element_type=F32)
                s = jnp.where(valid, s, NEG_INF)
                h0 = 4 * kv + half
                sink = jnp.where(row < CS, sink_ref[h0], sink_ref[h0 + 2])
                m = jnp.maximum(jnp.max(s, axis=-1, keepdims=True), sink)
                e = jnp.exp(s - m)
                denom = jnp.sum(e, axis=-1, keepdims=True) + jnp.exp(sink - m)
                o = jnp.dot(e.astype(BF16), vv, preferred_element_type=F32) / denom
                out = o if out is None else out + o
            oc_ref[0, lo:lo + CS, 2 * kv * LANES:(2 * kv + 1) * LANES] = out[:CS].astype(BF16)
            oc_ref[0, lo:lo + CS, (2 * kv + 1) * LANES:(2 * kv + 2) * LANES] = out[CS:].astype(BF16)

    kext[0:WINDOW, :] = kext[T:T + WINDOW, :]
    vext[0:WINDOW, :] = vext[T:T + WINDOW, :]


def _inproj_attn(x, g1, w_in, cos_t, sin_t, cache_k, cache_v, sinks, *, pos_base, T, CS):
    NB, L, _ = x.shape
    n_out = min(L, WINDOW)
    ext_rows = T - CS + KEY_WIN
    kern = functools.partial(_inproj_attn_kernel, pos_base, T, CS)
    return pl.pallas_call(
        kern,
        grid=(NB, L // T),
        in_specs=[
            pl.BlockSpec((1, T, D_MODEL), lambda b, t: (b, t, 0)),
            _const_spec((1, D_MODEL)),
            _const_spec((D_MODEL, D_IN)),
            pl.BlockSpec((T, LANES), lambda b, t: (t, 0)),
            pl.BlockSpec((T, LANES), lambda b, t: (t, 0)),
            pl.BlockSpec((1, WINDOW, D_KV), lambda b, t: (b, 0, 0)),
            pl.BlockSpec((1, WINDOW, D_KV), lambda b, t: (b, 0, 0)),
            pl.BlockSpec(memory_space=pltpu.SMEM),
        ],
        out_specs=[
            pl.BlockSpec((T, D_AB), lambda b, t: (t, b)),
            pl.BlockSpec((1, T, D_ATT), lambda b, t: (b, t, 0)),
            pl.BlockSpec((1, n_out, D_KV), lambda b, t: (b, 0, 0)),
            pl.BlockSpec((1, n_out, D_KV), lambda b, t: (b, 0, 0)),
        ],
        out_shape=[
            jax.ShapeDtypeStruct((L, NB * D_AB), F32),
            jax.ShapeDtypeStruct((NB, L, D_ATT), BF16),
            jax.ShapeDtypeStruct((NB, n_out, D_KV), F32),
            jax.ShapeDtypeStruct((NB, n_out, D_KV), F32),
        ],
        scratch_shapes=[pltpu.VMEM((ext_rows, D_KV), F32), pltpu.VMEM((ext_rows, D_KV), F32)],
        compiler_params=pltpu.CompilerParams(dimension_semantics=("arbitrary", "arbitrary"),
                                             vmem_limit_bytes=VMEM_LIMIT),
        name="inproj_attn",
    )(x, g1, w_in, cos_t, sin_t, cache_k, cache_v, sinks)


def _lru_s5_kernel(NB, T, pab_ref, cbuf0_ref, h0_ref, sr0_ref, si0_ref,
                   convw_ref, convb_ref, wgate_ref, bgate_ref, lamc_ref,
                   abar_ref, bb_ref, cc_ref, d_ref, wglu_ref, bglu_ref,
                   oab_ref, cbuf_out, h_out, sr_out, si_out,
                   uext, a_s, b_s, bu_s, h_st, s_st):
    tc = pl.program_id(0)
    R = T * NB
    TAIL = (CONV_W - 1) * NB

    @pl.when(tc == 0)
    def _():
        uext[0:TAIL, :] = cbuf0_ref[...]
        h_st[...] = h0_ref[...]
        s_st[:, 0:D_STATE] = sr0_ref[...]
        s_st[:, D_STATE:] = si0_ref[...]

    uext[TAIL:TAIL + R, :] = pab_ref[:, 0:D_LRU]
    xc = convb_ref[...]
    for j in range(CONV_W):
        xc = xc + uext[j * NB:j * NB + R, :] * convw_ref[j:j + 1, :]
    tail = uext[R:R + TAIL, :]
    uext[0:TAIL, :] = tail
    cbuf_out[...] = tail

    g = jnp.dot(xc.astype(BF16), wgate_ref[...], preferred_element_type=F32) + bgate_ref[...]
    r_gate = jax.nn.sigmoid(g[:, :D_LRU])
    i_gate = jax.nn.sigmoid(g[:, D_LRU:])
    log_a = r_gate * lamc_ref[...]
    a_s[...] = jnp.exp(log_a)
    b_s[...] = jnp.sqrt(1.0 - jnp.exp(2.0 * log_a)) * (i_gate * xc)

    def lru_step(t, h):
        r0 = pl.multiple_of(t * NB, NB)
        h = a_s[pl.ds(r0, NB), :] * h + b_s[pl.ds(r0, NB), :]
        b_s[pl.ds(r0, NB), :] = h
        return h

    h_last = lax.fori_loop(0, T, lru_step, h_st[...])
    h_st[...] = h_last
    h_out[...] = h_last
    oab_ref[:, 0:D_LRU] = (b_s[...] * jax.nn.gelu(pab_ref[:, D_LRU:2 * D_LRU])).astype(BF16)

    u_b = pab_ref[:, 2 * D_LRU:D_AB]
    bu_s[...] = jnp.dot(u_b.astype(BF16), bb_ref[...], preferred_element_type=F32)

    for ch in range(D_STATE // SCAN_LANES):
        c0 = ch * SCAN_LANES
        ar = jnp.broadcast_to(abar_ref[0:1, c0:c0 + SCAN_LANES], (NB, SCAN_LANES))
        ai = jnp.broadcast_to(abar_ref[1:2, c0:c0 + SCAN_LANES], (NB, SCAN_LANES))

        def s5_step(t, carry, c0=c0, ar=ar, ai=ai):
            sr, si = carry
            r0 = pl.multiple_of(t * NB, NB)
            nr = ar * sr - ai * si + bu_s[pl.ds(r0, NB), c0:c0 + SCAN_LANES]
            ni = ar * si + ai * sr + bu_s[pl.ds(r0, NB), D_STATE + c0:D_STATE + c0 + SCAN_LANES]
            bu_s[pl.ds(r0, NB), c0:c0 + SCAN_LANES] = nr
            bu_s[pl.ds(r0, NB), D_STATE + c0:D_STATE + c0 + SCAN_LANES] = ni
            return nr, ni

        sr, si = lax.fori_loop(0, T, s5_step,
                               (s_st[:, c0:c0 + SCAN_LANES], s_st[:, D_STATE + c0:D_STATE + c0 + SCAN_LANES]))
        s_st[:, c0:c0 + SCAN_LANES] = sr
        s_st[:, D_STATE + c0:D_STATE + c0 + SCAN_LANES] = si

    sr_out[...] = s_st[:, 0:D_STATE]
    si_out[...] = s_st[:, D_STATE:]

    y = jnp.dot(bu_s[...].astype(BF16), cc_ref[...], preferred_element_type=F32) + d_ref[...] * u_b
    z = jax.nn.gelu(y)
    gl = jnp.dot(z.astype(BF16), wglu_ref[...], preferred_element_type=F32) + bglu_ref[...]
    oab_ref[:, D_LRU:] = (z * jax.nn.sigmoid(gl)).astype(BF16)


def _lru_s5(pab, cbuf0, h0, sr0, si0, convw, convb, wgate, bgate, lamc, abar, bb, cc, d, wglu, bglu, *, NB, T):
    rows = pab.shape[0]
    L = rows // NB
    R = T * NB
    TAIL = (CONV_W - 1) * NB
    kern = functools.partial(_lru_s5_kernel, NB, T)
    consts = [cbuf0, h0, sr0, si0, convw, convb, wgate, bgate, lamc, abar, bb, cc, d, wglu, bglu]
    return pl.pallas_call(
        kern,
        grid=(L // T,),
        in_specs=[pl.BlockSpec((R, D_AB), lambda t: (t, 0))] + [_const_spec(c.shape) for c in consts],
        out_specs=[
            pl.BlockSpec((R, 2 * D_LRU), lambda t: (t, 0)),
            pl.BlockSpec((TAIL, D_LRU), lambda t: (0, 0)),
            pl.BlockSpec((NB, D_LRU), lambda t: (0, 0)),
            pl.BlockSpec((NB, D_STATE), lambda t: (0, 0)),
            pl.BlockSpec((NB, D_STATE), lambda t: (0, 0)),
        ],
        out_shape=[
            jax.ShapeDtypeStruct((rows, 2 * D_LRU), BF16),
            jax.ShapeDtypeStruct((TAIL, D_LRU), F32),
            jax.ShapeDtypeStruct((NB, D_LRU), F32),
            jax.ShapeDtypeStruct((NB, D_STATE), F32),
            jax.ShapeDtypeStruct((NB, D_STATE), F32),
        ],
        scratch_shapes=[
            pltpu.VMEM((R + TAIL, D_LRU), F32),
            pltpu.VMEM((R, D_LRU), F32),
            pltpu.VMEM((R, D_LRU), F32),
            pltpu.VMEM((R, 2 * D_STATE), F32),
            pltpu.VMEM((NB, D_LRU), F32),
            pltpu.VMEM((NB, 2 * D_STATE), F32),
        ],
        compiler_params=pltpu.CompilerParams(dimension_semantics=("arbitrary",), vmem_limit_bytes=VMEM_LIMIT),
        name="lru_s5",
    )(pab, *consts)


def _outproj_mlp_kernel(final, x_ref, oab_ref, oc_ref, wout_ref, g2_ref, wup_ref, wdn_ref, gf_ref, y_ref):
    x = x_ref[0]
    mix = jnp.dot(oab_ref[...], wout_ref[0:2 * D_LRU, :], preferred_element_type=F32)
    mix = mix + jnp.dot(oc_ref[0], wout_ref[2 * D_LRU:, :], preferred_element_type=F32)
    x = x + mix
    hm = _rms(x, g2_ref[...]).astype(BF16)
    down = None
    for c in range(D_FF // FF_CHUNK):
        up = jnp.dot(hm, wup_ref[:, c * FF_CHUNK:(c + 1) * FF_CHUNK], preferred_element_type=F32)
        act = jnp.square(jnp.maximum(up, 0.0)).astype(BF16)
        part = jnp.dot(act, wdn_ref[c * FF_CHUNK:(c + 1) * FF_CHUNK, :], preferred_element_type=F32)
        down = part if down is None else down + part
    x = x + down
    if final:
        x = _rms(x, gf_ref[...])
    y_ref[0] = x


def _outproj_mlp(x, oab, oc, w_out, g2, w_up, w_down, gf, *, final, T):
    NB, L, _ = x.shape
    kern = functools.partial(_outproj_mlp_kernel, final)
    return pl.pallas_call(
        kern,
        grid=(NB, L // T),
        in_specs=[
            pl.BlockSpec((1, T, D_MODEL), lambda b, t: (b, t, 0)),
            pl.BlockSpec((T, 2 * D_LRU), lambda b, t: (t, b)),
            pl.BlockSpec((1, T, D_ATT), lambda b, t: (b, t, 0)),
            _const_spec((D_MODEL, D_MODEL)),
            _const_spec((1, D_MODEL)),
            _const_spec((D_MODEL, D_FF)),
            _const_spec((D_FF, D_MODEL)),
            _const_spec((1, D_MODEL)),
        ],
        out_specs=pl.BlockSpec((1, T, D_MODEL), lambda b, t: (b, t, 0)),
        out_shape=jax.ShapeDtypeStruct((NB, L, D_MODEL), F32),
        compiler_params=pltpu.CompilerParams(dimension_semantics=("arbitrary", "arbitrary"),
                                             vmem_limit_bytes=VMEM_LIMIT),
        name="outproj_mlp",
    )(x, oab, oc, w_out, g2, w_up, w_down, gf)


def _block_diag(blocks):
    n = blocks.shape[0]
    rows = []
    for i in range(n):
        rows.append(jnp.concatenate(
            [blocks[i] if j == i else jnp.zeros((blocks[i].shape[0], blocks[j].shape[1]), blocks.dtype)
             for j in range(n)], axis=1))
    return jnp.concatenate(rows, axis=0)


def _rope_tables(pos_base, L):
    half = HEAD_DIM // 2
    inv = ROPE_THETA ** (-jnp.arange(half, dtype=F32) / half)
    ang = (pos_base + jnp.arange(L, dtype=jnp.int32)).astype(F32)[:, None] * inv[None, :]
    cos, sin = jnp.cos(ang), jnp.sin(ang)
    reps = LANES // HEAD_DIM
    return (jnp.tile(jnp.concatenate([cos, cos], axis=-1), (1, reps)),
            jnp.tile(jnp.concatenate([-sin, sin], axis=-1), (1, reps)))


def _layer_params(l, norm1, w_in, conv_w, conv_b, w_rg, b_rg, w_ig, b_ig, lru_lambda,
                  ssm_a_re, ssm_a_im, ssm_b_re, ssm_b_im, ssm_c_re, ssm_c_im, ssm_d, ssm_log_dt,
                  w_glu, b_glu, attn_sinks, w_out, norm2, w_up, w_down):
    dt = jnp.exp(ssm_log_dt[l])[:, None]
    lr, li = ssm_a_re[l], ssm_a_im[l]
    mag = jnp.exp(lr * dt)
    abar_r, abar_i = mag * jnp.cos(li * dt), mag * jnp.sin(li * dt)
    den = lr * lr + li * li
    nr = abar_r - 1.0
    fr = (nr * lr + abar_i * li) / den
    fi = (abar_i * lr - nr * li) / den
    br, bi = ssm_b_re[l], ssm_b_im[l]
    bb_r = fr[..., None] * br - fi[..., None] * bi
    bb_i = fr[..., None] * bi + fi[..., None] * br
    bb = jnp.concatenate([_block_diag(jnp.swapaxes(bb_r, 1, 2)), _block_diag(jnp.swapaxes(bb_i, 1, 2))], axis=1)
    cc = jnp.concatenate([_block_diag(jnp.swapaxes(ssm_c_re[l], 1, 2)),
                          -_block_diag(jnp.swapaxes(ssm_c_im[l], 1, 2))], axis=0)
    return dict(
        g1=norm1[l][None, :], w_in=w_in[l].astype(BF16),
        convw=conv_w[l], convb=conv_b[l][None, :],
        wgate=jnp.concatenate([_block_diag(w_rg[l]), _block_diag(w_ig[l])], axis=1).astype(BF16),
        bgate=jnp.concatenate([b_rg[l], b_ig[l]])[None, :],
        lamc=(-LRU_C * jax.nn.softplus(-lru_lambda[l]))[None, :],
        abar=jnp.stack([abar_r.reshape(-1), abar_i.reshape(-1)], axis=0),
        bb=bb.astype(BF16), cc=cc.astype(BF16), d=ssm_d[l][None, :],
        wglu=w_glu[l].astype(BF16), bglu=b_glu[l][None, :],
        sinks=attn_sinks[l], w_out=w_out[l].astype(BF16), g2=norm2[l][None, :],
        w_up=w_up[l].astype(BF16), w_down=w_down[l].astype(BF16),
    )


def _layer(x, p, tables, conv_buf, h0, s_re, s_im, cache_k, cache_v, gf, *, pos_base, final, CS):
    NB, L, _ = x.shape
    t_attn, t_scan, t_mlp = min(T_ATTN, L), min(T_SCAN, L), min(T_MLP, L)
    pab, oc, k_new, v_new = _inproj_attn(x, p['g1'], p['w_in'], tables[0], tables[1], cache_k, cache_v, p['sinks'],
                                         pos_base=pos_base, T=t_attn, CS=CS)
    cbuf0 = jnp.swapaxes(conv_buf, 0, 1).reshape((CONV_W - 1) * NB, D_LRU)
    oab, cbuf, h_last, sr, si = _lru_s5(
        pab.reshape(L * NB, D_AB), cbuf0, h0, s_re.reshape(NB, D_STATE), s_im.reshape(NB, D_STATE),
        p['convw'], p['convb'], p['wgate'], p['bgate'], p['lamc'], p['abar'], p['bb'], p['cc'], p['d'],
        p['wglu'], p['bglu'], NB=NB, T=t_scan)
    x_new = _outproj_mlp(x, oab.reshape(L, NB * 2 * D_LRU), oc, p['w_out'], p['g2'], p['w_up'], p['w_down'], gf,
                         final=final, T=t_mlp)
    n_out = k_new.shape[1]
    state = (jnp.swapaxes(cbuf.reshape(CONV_W - 1, NB, D_LRU), 0, 1), h_last,
             sr.reshape(NB, SSM_GROUPS, SSM_STATE), si.reshape(NB, SSM_GROUPS, SSM_STATE),
             k_new.reshape(NB, n_out, N_KV_HEADS, HEAD_DIM), v_new.reshape(NB, n_out, N_KV_HEADS, HEAD_DIM))
    return x_new, state


def kernel(x_prompt, x_sample, cache_conv_a, state_lru, state_ssm_re, state_ssm_im, cache_k, cache_v, norm1, w_in, conv_w, conv_b, w_rg, b_rg, w_ig, b_ig, lru_lambda, ssm_a_re, ssm_a_im, ssm_b_re, ssm_b_im, ssm_c_re, ssm_c_im, ssm_d, ssm_log_dt, w_glu, b_glu, attn_sinks, w_out, norm2, w_up, w_down, norm_f):
    Bp, Lp = x_prompt.shape[0], x_prompt.shape[1]
    Bs, Ls = x_sample.shape[0], x_sample.shape[1]
    depth = w_in.shape[0]
    tab_p = _rope_tables(0, Lp)
    tab_s = _rope_tables(PAST_LEN, Ls)
    gf = norm_f[None, :]
    zconv = jnp.zeros((Bp, CONV_W - 1, D_LRU), F32)
    zh = jnp.zeros((Bp, D_LRU), F32)
    zs = jnp.zeros((Bp, SSM_GROUPS, SSM_STATE), F32)
    zkv = jnp.zeros((Bp, WINDOW, D_KV), F32)

    xp, xs = x_prompt, x_sample
    st_p, st_s = [], []
    for l in range(depth):
        p = _layer_params(l, norm1, w_in, conv_w, conv_b, w_rg, b_rg, w_ig, b_ig, lru_lambda,
                          ssm_a_re, ssm_a_im, ssm_b_re, ssm_b_im, ssm_c_re, ssm_c_im, ssm_d, ssm_log_dt,
                          w_glu, b_glu, attn_sinks, w_out, norm2, w_up, w_down)
        final = l == depth - 1
        xp, sp = _layer(xp, p, tab_p, zconv, zh, zs, zs, zkv, zkv, gf, pos_base=0, final=final, CS=CHUNK)
        xs, ss = _layer(xs, p, tab_s, cache_conv_a[l], state_lru[l], state_ssm_re[l], state_ssm_im[l],
                        cache_k[l].reshape(Bs, WINDOW, D_KV), cache_v[l].reshape(Bs, WINDOW, D_KV), gf,
                        pos_base=PAST_LEN, final=final, CS=Ls)
        st_p.append(sp)
        st_s.append(ss)

    outs = [xp, xs]
    for sts in (st_p, st_s):
        for i in range(6):
            outs.append(jnp.stack([s[i] for s in sts], 0))
    return tuple(outs)
```

```python
import functools

import jax
import jax.numpy as jnp
from jax import lax
from jax.experimental import pallas as pl
from jax.experimental.pallas import tpu as pltpu

F32 = jnp.float32
BF16 = jnp.bfloat16

D_MODEL = 1024
PAST_LEN = 4096
CHUNK = 64
D_LRU = 256
CONV_W = 4
LRU_C = 8.0
D_SSM = 256
SSM_GROUPS = 16
SSM_STATE = 64
D_STATE = SSM_GROUPS * SSM_STATE
HEAD_DIM = 64
N_Q_HEADS = 8
N_KV_HEADS = 2
D_ATT = N_Q_HEADS * HEAD_DIM
D_KV = N_KV_HEADS * HEAD_DIM
WINDOW = 128
ROPE_THETA = 10000.0
D_AB = 2 * D_LRU + D_SSM
D_IN = D_AB + D_ATT + 2 * D_KV
D_FF = 4 * D_MODEL
RMS_EPS = 1e-6
NEG_INF = -1e30

LANES = 128
KEY_WIN = 2 * LANES
FF_CHUNK = 1024
SCAN_LANES = 256
SCAN_SUB = 16
VMEM_LIMIT = 56 * 1024 * 1024

T_ATTN = 256
T_SCAN = 64
T_MLP = 512


def _rms(x, g):
    return (x * lax.rsqrt(jnp.mean(x * x, axis=-1, keepdims=True) + RMS_EPS)) * g


def _const_spec(shape):
    nd = len(shape)
    return pl.BlockSpec(shape, lambda *_: (0,) * nd, pipeline_mode=pl.Buffered(1))


def _inproj_attn_kernel(pos_base, T, CS, x_ref, g1_ref, win_ref, cos_ref, sin_ref, ck_ref, cv_ref, sink_ref,
                        pab_ref, oc_ref, kout_ref, vout_ref, kext, vext):
    tc = pl.program_id(1)

    @pl.when(tc == 0)
    def _():
        kext[...] = jnp.zeros_like(kext)
        vext[...] = jnp.zeros_like(vext)
        kext[0:WINDOW, :] = ck_ref[0]
        vext[0:WINDOW, :] = cv_ref[0]

    hn = _rms(x_ref[0], g1_ref[...])
    proj = jnp.dot(hn.astype(BF16), win_ref[...], preferred_element_type=F32)
    pab_ref[0] = proj[:, :D_AB]

    cos = cos_ref[...]
    sin = sin_ref[...]
    lane = lax.broadcasted_iota(jnp.int32, (1, LANES), 1)
    first_half = (lane % HEAD_DIM) < (HEAD_DIM // 2)
    low_head = lane < HEAD_DIM

    def rope(a):
        partner = jnp.where(first_half, pltpu.roll(a, LANES - HEAD_DIM // 2, 1), pltpu.roll(a, HEAD_DIM // 2, 1))
        return a * cos + partner * sin

    scale = HEAD_DIM ** -0.5
    qcols = [(rope(proj[:, D_AB + j * LANES:D_AB + (j + 1) * LANES]) * scale).astype(BF16)
             for j in range(D_ATT // LANES)]
    k_rot = rope(proj[:, D_AB + D_ATT:D_AB + D_ATT + D_KV])
    v_new = proj[:, D_AB + D_ATT + D_KV:]
    kext[WINDOW:WINDOW + T, :] = k_rot
    vext[WINDOW:WINDOW + T, :] = v_new
    n_out = kout_ref.shape[1]
    kout_ref[0] = kext[WINDOW + T - n_out:WINDOW + T, :]
    vout_ref[0] = vext[WINDOW + T - n_out:WINDOW + T, :]

    def variants(full):
        swapped = pltpu.roll(full, HEAD_DIM, 1)
        zero = jnp.zeros_like(full)
        return ((jnp.where(low_head, full, zero).astype(BF16), jnp.where(low_head, zero, swapped).astype(BF16)),
                (jnp.where(low_head, swapped, zero).astype(BF16), jnp.where(low_head, zero, full).astype(BF16)))

    kvar = variants(kext[...])
    vvar = variants(vext[...])

    kidx = lax.broadcasted_iota(jnp.int32, (1, KEY_WIN), 1)
    row = lax.broadcasted_iota(jnp.int32, (2 * CS, 1), 0)
    for c in range(T // CS):
        lo = c * CS
        pos0 = pos_base + tc * T + lo
        valid = (kidx >= WINDOW - pos0) & (kidx < WINDOW + CS)
        for kv in range(N_KV_HEADS):
            q2 = jnp.concatenate([qcols[2 * kv][lo:lo + CS], qcols[2 * kv + 1][lo:lo + CS]], axis=0)
            out = None
            for half in range(2):
                kk = kvar[kv][half][lo:lo + KEY_WIN]
                vv = vvar[kv][half][lo:lo + KEY_WIN]
                s = lax.dot_general(q2, kk, (((1,), (1,)), ((), ())), preferred_element_type=F32)
                s = jnp.where(valid, s, NEG_INF)
                h0 = 4 * kv + half
                sink = jnp.where(row < CS, sink_ref[h0], sink_ref[h0 + 2])
                m = jnp.maximum(jnp.max(s, axis=-1, keepdims=True), sink)
                e = jnp.exp(s - m)
                denom = jnp.sum(e, axis=-1, keepdims=True) + jnp.exp(sink - m)
                o = jnp.dot(e.astype(BF16), vv, preferred_element_type=F32) / denom
                out = o if out is None else out + o
            oc_ref[0, lo:lo + CS, 2 * kv * LANES:(2 * kv + 1) * LANES] = out[:CS].astype(BF16)
            oc_ref[0, lo:lo + CS, (2 * kv + 1) * LANES:(2 * kv + 2) * LANES] = out[CS:].astype(BF16)

    kext[0:WINDOW, :] = kext[T:T + WINDOW, :]
    vext[0:WINDOW, :] = vext[T:T + WINDOW, :]


def _inproj_attn(x, g1, w_in, cos_t, sin_t, cache_k, cache_v, sinks, *, pos_base, T, CS):
    NB, L, _ = x.shape
    n_out = min(L, WINDOW)
    ext_rows = T - CS + KEY_WIN
    kern = functools.partial(_inproj_attn_kernel, pos_base, T, CS)
    return pl.pallas_call(
        kern,
        grid=(NB, L // T),
        in_specs=[
            pl.BlockSpec((1, T, D_MODEL), lambda b, t: (b, t, 0)),
            _const_spec((1, D_MODEL)),
            _const_spec((D_MODEL, D_IN)),
            pl.BlockSpec((T, LANES), lambda b, t: (t, 0)),
            pl.BlockSpec((T, LANES), lambda b, t: (t, 0)),
            pl.BlockSpec((1, WINDOW, D_KV), lambda b, t: (b, 0, 0)),
            pl.BlockSpec((1, WINDOW, D_KV), lambda b, t: (b, 0, 0)),
            pl.BlockSpec(memory_space=pltpu.SMEM),
        ],
        out_specs=[
            pl.BlockSpec((1, T, D_AB), lambda b, t: (b, t, 0)),
            pl.BlockSpec((1, T, D_ATT), lambda b, t: (b, t, 0)),
            pl.BlockSpec((1, n_out, D_KV), lambda b, t: (b, 0, 0)),
            pl.BlockSpec((1, n_out, D_KV), lambda b, t: (b, 0, 0)),
        ],
        out_shape=[
            jax.ShapeDtypeStruct((NB, L, D_AB), F32),
            jax.ShapeDtypeStruct((NB, L, D_ATT), BF16),
            jax.ShapeDtypeStruct((NB, n_out, D_KV), F32),
            jax.ShapeDtypeStruct((NB, n_out, D_KV), F32),
        ],
        scratch_shapes=[pltpu.VMEM((ext_rows, D_KV), F32), pltpu.VMEM((ext_rows, D_KV), F32)],
        compiler_params=pltpu.CompilerParams(dimension_semantics=("arbitrary", "arbitrary"),
                                             vmem_limit_bytes=VMEM_LIMIT),
        name="inproj_attn",
    )(x, g1, w_in, cos_t, sin_t, cache_k, cache_v, sinks)


def _lru_s5_kernel(NB, T, TS, pab_ref, cbuf0_ref, h0_ref, sr0_ref, si0_ref,
                   convw_ref, convb_ref, wgate_ref, bgate_ref, lamc_ref,
                   abar_ref, bb_ref, cc_ref, d_ref, wglu_ref, bglu_ref,
                   oab_ref, cbuf_out, h_out, sr_out, si_out,
                   uext, gu_s, a_s, b_s, bu_s, h_st, s_st):
    tc = pl.program_id(0)
    R = T * NB
    RS = TS * NB
    TAIL = (CONV_W - 1) * NB

    @pl.when(tc == 0)
    def _():
        uext[0:TAIL, :] = cbuf0_ref[...]
        h_st[...] = h0_ref[...]
        s_st[:, 0:D_STATE] = sr0_ref[...]
        s_st[:, D_STATE:] = si0_ref[...]

    def to_time_major(t0, c0):
        return jnp.swapaxes(pab_ref[:, t0:t0 + TS, c0:c0 + D_LRU], 0, 1).reshape(RS, D_LRU)

    for sb in range(T // TS):
        r0 = sb * RS
        uext[TAIL + r0:TAIL + r0 + RS, :] = to_time_major(sb * TS, 0)
        gu_s[r0:r0 + RS, 0:D_LRU] = to_time_major(sb * TS, D_LRU)
        gu_s[r0:r0 + RS, D_LRU:] = to_time_major(sb * TS, 2 * D_LRU)

    for sb in range(T // TS):
        r0 = sb * RS
        xc = convb_ref[...]
        for j in range(CONV_W):
            xc = xc + uext[j * NB + r0:j * NB + r0 + RS, :] * convw_ref[j:j + 1, :]
        g = jnp.dot(xc.astype(BF16), wgate_ref[...], preferred_element_type=F32) + bgate_ref[...]
        r_gate = jax.nn.sigmoid(g[:, :D_LRU])
        i_gate = jax.nn.sigmoid(g[:, D_LRU:])
        log_a = r_gate * lamc_ref[...]
        a_s[r0:r0 + RS, :] = jnp.exp(log_a)
        b_s[r0:r0 + RS, :] = jnp.sqrt(1.0 - jnp.exp(2.0 * log_a)) * (i_gate * xc)
        bu_s[r0:r0 + RS, :] = jnp.dot(gu_s[r0:r0 + RS, D_LRU:].astype(BF16), bb_ref[...],
                                      preferred_element_type=F32)

    tail = uext[R:R + TAIL, :]
    uext[0:TAIL, :] = tail
    cbuf_out[...] = tail

    def lru_step(t, h):
        r0 = pl.multiple_of(t * NB, NB)
        h = a_s[pl.ds(r0, NB), :] * h + b_s[pl.ds(r0, NB), :]
        b_s[pl.ds(r0, NB), :] = h
        return h

    h_last = lax.fori_loop(0, T, lru_step, h_st[...])
    h_st[...] = h_last
    h_out[...] = h_last

    for ch in range(D_STATE // SCAN_LANES):
        c0 = ch * SCAN_LANES
        ar = jnp.broadcast_to(abar_ref[0:1, c0:c0 + SCAN_LANES], (NB, SCAN_LANES))
        ai = jnp.broadcast_to(abar_ref[1:2, c0:c0 + SCAN_LANES], (NB, SCAN_LANES))

        def s5_step(t, carry, c0=c0, ar=ar, ai=ai):
            sr, si = carry
            r0 = pl.multiple_of(t * NB, NB)
            nr = ar * sr - ai * si + bu_s[pl.ds(r0, NB), c0:c0 + SCAN_LANES]
            ni = ar * si + ai * sr + bu_s[pl.ds(r0, NB), D_STATE + c0:D_STATE + c0 + SCAN_LANES]
            bu_s[pl.ds(r0, NB), c0:c0 + SCAN_LANES] = nr
            bu_s[pl.ds(r0, NB), D_STATE + c0:D_STATE + c0 + SCAN_LANES] = ni
            return nr, ni

        sr, si = lax.fori_loop(0, T, s5_step,
                               (s_st[:, c0:c0 + SCAN_LANES], s_st[:, D_STATE + c0:D_STATE + c0 + SCAN_LANES]))
        s_st[:, c0:c0 + SCAN_LANES] = sr
        s_st[:, D_STATE + c0:D_STATE + c0 + SCAN_LANES] = si

    sr_out[...] = s_st[:, 0:D_STATE]
    si_out[...] = s_st[:, D_STATE:]

    def to_batch_major(v):
        return jnp.swapaxes(v.reshape(TS, NB, D_LRU), 0, 1).astype(BF16)

    for sb in range(T // TS):
        r0 = sb * RS
        t0 = sb * TS
        out_a = b_s[r0:r0 + RS, :] * jax.nn.gelu(gu_s[r0:r0 + RS, 0:D_LRU])
        oab_ref[:, t0:t0 + TS, 0:D_LRU] = to_batch_major(out_a)
        y = jnp.dot(bu_s[r0:r0 + RS, :].astype(BF16), cc_ref[...], preferred_element_type=F32)
        y = y + d_ref[...] * gu_s[r0:r0 + RS, D_LRU:]
        z = jax.nn.gelu(y)
        gl = jnp.dot(z.astype(BF16), wglu_ref[...], preferred_element_type=F32) + bglu_ref[...]
        oab_ref[:, t0:t0 + TS, D_LRU:] = to_batch_major(z * jax.nn.sigmoid(gl))


def _lru_s5(pab, cbuf0, h0, sr0, si0, convw, convb, wgate, bgate, lamc, abar, bb, cc, d, wglu, bglu, *, T):
    NB, L, _ = pab.shape
    R = T * NB
    TAIL = (CONV_W - 1) * NB
    TS = min(T, SCAN_SUB)
    kern = functools.partial(_lru_s5_kernel, NB, T, TS)
    consts = [cbuf0, h0, sr0, si0, convw, convb, wgate, bgate, lamc, abar, bb, cc, d, wglu, bglu]
    return pl.pallas_call(
        kern,
        grid=(L // T,),
        in_specs=[pl.BlockSpec((NB, T, D_AB), lambda t: (0, t, 0))] + [_const_spec(c.shape) for c in consts],
        out_specs=[
            pl.BlockSpec((NB, T, 2 * D_LRU), lambda t: (0, t, 0)),
            pl.BlockSpec((TAIL, D_LRU), lambda t: (0, 0)),
            pl.BlockSpec((NB, D_LRU), lambda t: (0, 0)),
            pl.BlockSpec((NB, D_STATE), lambda t: (0, 0)),
            pl.BlockSpec((NB, D_STATE), lambda t: (0, 0)),
        ],
        out_shape=[
            jax.ShapeDtypeStruct((NB, L, 2 * D_LRU), BF16),
            jax.ShapeDtypeStruct((TAIL, D_LRU), F32),
            jax.ShapeDtypeStruct((NB, D_LRU), F32),
            jax.ShapeDtypeStruct((NB, D_STATE), F32),
            jax.ShapeDtypeStruct((NB, D_STATE), F32),
        ],
        scratch_shapes=[
            pltpu.VMEM((R + TAIL, D_LRU), F32),
            pltpu.VMEM((R, 2 * D_LRU), F32),
            pltpu.VMEM((R, D_LRU), F32),
            pltpu.VMEM((R, D_LRU), F32),
            pltpu.VMEM((R, 2 * D_STATE), F32),
            pltpu.VMEM((NB, D_LRU), F32),
            pltpu.VMEM((NB, 2 * D_STATE), F32),
        ],
        compiler_params=pltpu.CompilerParams(dimension_semantics=("arbitrary",), vmem_limit_bytes=VMEM_LIMIT),
        name="lru_s5",
    )(pab, *consts)


def _outproj_mlp_kernel(final, x_ref, oab_ref, oc_ref, wout_ref, g2_ref, wup_ref, wdn_ref, gf_ref, y_ref):
    x = x_ref[0]
    mix = jnp.dot(oab_ref[0], wout_ref[0:2 * D_LRU, :], preferred_element_type=F32)
    mix = mix + jnp.dot(oc_ref[0], wout_ref[2 * D_LRU:, :], preferred_element_type=F32)
    x = x + mix
    hm = _rms(x, g2_ref[...]).astype(BF16)
    down = None
    for c in range(D_FF // FF_CHUNK):
        up = jnp.dot(hm, wup_ref[:, c * FF_CHUNK:(c + 1) * FF_CHUNK], preferred_element_type=F32)
        act = jnp.square(jnp.maximum(up, 0.0)).astype(BF16)
        part = jnp.dot(act, wdn_ref[c * FF_CHUNK:(c + 1) * FF_CHUNK, :], preferred_element_type=F32)
        down = part if down is None else down + part
    x = x + down
    if final:
        x = _rms(x, gf_ref[...])
    y_ref[0] = x


def _outproj_mlp(x, oab, oc, w_out, g2, w_up, w_down, gf, *, final, T):
    NB, L, _ = x.shape
    kern = functools.partial(_outproj_mlp_kernel, final)
    row_block = lambda width: pl.BlockSpec((1, T, width), lambda b, t: (b, t, 0))
    return pl.pallas_call(
        kern,
        grid=(NB, L // T),
        in_specs=[
            row_block(D_MODEL), row_block(2 * D_LRU), row_block(D_ATT),
            _const_spec((D_MODEL, D_MODEL)),
            _const_spec((1, D_MODEL)),
            _const_spec((D_MODEL, D_FF)),
            _const_spec((D_FF, D_MODEL)),
            _const_spec((1, D_MODEL)),
        ],
        out_specs=row_block(D_MODEL),
        out_shape=jax.ShapeDtypeStruct((NB, L, D_MODEL), F32),
        compiler_params=pltpu.CompilerParams(dimension_semantics=("arbitrary", "arbitrary"),
                                             vmem_limit_bytes=VMEM_LIMIT),
        name="outproj_mlp",
    )(x, oab, oc, w_out, g2, w_up, w_down, gf)


def _block_diag(blocks):
    n, r, c = blocks.shape
    on_diag = jnp.eye(n, dtype=bool)[:, None, :, None]
    return jnp.where(on_diag, blocks[:, :, None, :], jnp.zeros((), blocks.dtype)).reshape(n * r, n * c)


def _rope_tables(pos_base, L):
    half = HEAD_DIM // 2
    inv = ROPE_THETA ** (-jnp.arange(half, dtype=F32) / half)
    ang = (pos_base + jnp.arange(L, dtype=jnp.int32)).astype(F32)[:, None] * inv[None, :]
    cos, sin = jnp.cos(ang), jnp.sin(ang)
    reps = LANES // HEAD_DIM
    return (jnp.tile(jnp.concatenate([cos, cos], axis=-1), (1, reps)),
            jnp.tile(jnp.concatenate([-sin, sin], axis=-1), (1, reps)))


def _layer_params(l, norm1, w_in, conv_w, conv_b, w_rg, b_rg, w_ig, b_ig, lru_lambda,
                  ssm_a_re, ssm_a_im, ssm_b_re, ssm_b_im, ssm_c_re, ssm_c_im, ssm_d, ssm_log_dt,
                  w_glu, b_glu, attn_sinks, w_out, norm2, w_up, w_down):
    dt = jnp.exp(ssm_log_dt[l])[:, None]
    lr, li = ssm_a_re[l], ssm_a_im[l]
    mag = jnp.exp(lr * dt)
    abar_r, abar_i = mag * jnp.cos(li * dt), mag * jnp.sin(li * dt)
    den = lr * lr + li * li
    nr = abar_r - 1.0
    fr = (nr * lr + abar_i * li) / den
    fi = (abar_i * lr - nr * li) / den
    br, bi = ssm_b_re[l], ssm_b_im[l]
    bb_r = fr[..., None] * br - fi[..., None] * bi
    bb_i = fr[..., None] * bi + fi[..., None] * br
    bb = jnp.concatenate([_block_diag(jnp.swapaxes(bb_r, 1, 2)), _block_diag(jnp.swapaxes(bb_i, 1, 2))], axis=1)
    cc = jnp.concatenate([_block_diag(jnp.swapaxes(ssm_c_re[l], 1, 2)),
                          -_block_diag(jnp.swapaxes(ssm_c_im[l], 1, 2))], axis=0)
    return dict(
        g1=norm1[l][None, :], w_in=w_in[l].astype(BF16),
        convw=conv_w[l], convb=conv_b[l][None, :],
        wgate=jnp.concatenate([_block_diag(w_rg[l]), _block_diag(w_ig[l])], axis=1).astype(BF16),
        bgate=jnp.concatenate([b_rg[l], b_ig[l]])[None, :],
        lamc=(-LRU_C * jax.nn.softplus(-lru_lambda[l]))[None, :],
        abar=jnp.stack([abar_r.reshape(-1), abar_i.reshape(-1)], axis=0),
        bb=bb.astype(BF16), cc=cc.astype(BF16), d=ssm_d[l][None, :],
        wglu=w_glu[l].astype(BF16), bglu=b_glu[l][None, :],
        sinks=attn_sinks[l], w_out=w_out[l].astype(BF16), g2=norm2[l][None, :],
        w_up=w_up[l].astype(BF16), w_down=w_down[l].astype(BF16),
    )


def _layer(x, p, tables, conv_buf, h0, s_re, s_im, cache_k, cache_v, gf, *, pos_base, final, CS):
    NB, L, _ = x.shape
    pab, oc, k_new, v_new = _inproj_attn(x, p['g1'], p['w_in'], tables[0], tables[1], cache_k, cache_v, p['sinks'],
                                         pos_base=pos_base, T=min(T_ATTN, L), CS=CS)
    cbuf0 = jnp.swapaxes(conv_buf, 0, 1).reshape((CONV_W - 1) * NB, D_LRU)
    oab, cbuf, h_last, sr, si = _lru_s5(
        pab, cbuf0, h0, s_re.reshape(NB, D_STATE), s_im.reshape(NB, D_STATE),
        p['convw'], p['convb'], p['wgate'], p['bgate'], p['lamc'], p['abar'], p['bb'], p['cc'], p['d'],
        p['wglu'], p['bglu'], T=min(T_SCAN, L))
    if NB * L <= T_MLP:
        fold = lambda a: a.reshape(1, NB * L, a.shape[-1])
        x_new = _outproj_mlp(fold(x), fold(oab), fold(oc), p['w_out'], p['g2'], p['w_up'], p['w_down'], gf,
                             final=final, T=NB * L).reshape(NB, L, D_MODEL)
    else:
        x_new = _outproj_mlp(x, oab, oc, p['w_out'], p['g2'], p['w_up'], p['w_down'], gf,
                             final=final, T=min(T_MLP, L))
    n_out = k_new.shape[1]
    state = (jnp.swapaxes(cbuf.reshape(CONV_W - 1, NB, D_LRU), 0, 1), h_last,
             sr.reshape(NB, SSM_GROUPS, SSM_STATE), si.reshape(NB, SSM_GROUPS, SSM_STATE),
             k_new.reshape(NB, n_out, N_KV_HEADS, HEAD_DIM), v_new.reshape(NB, n_out, N_KV_HEADS, HEAD_DIM))
    return x_new, state


def kernel(x_prompt, x_sample, cache_conv_a, state_lru, state_ssm_re, state_ssm_im, cache_k, cache_v, norm1, w_in, conv_w, conv_b, w_rg, b_rg, w_ig, b_ig, lru_lambda, ssm_a_re, ssm_a_im, ssm_b_re, ssm_b_im, ssm_c_re, ssm_c_im, ssm_d, ssm_log_dt, w_glu, b_glu, attn_sinks, w_out, norm2, w_up, w_down, norm_f):
    Bp, Lp = x_prompt.shape[0], x_prompt.shape[1]
    Bs, Ls = x_sample.shape[0], x_sample.shape[1]
    depth = w_in.shape[0]
    tab_p = _rope_tables(0, Lp)
    tab_s = _rope_tables(PAST_LEN, Ls)
    gf = norm_f[None, :]
    zconv = jnp.zeros((Bp, CONV_W - 1, D_LRU), F32)
    zh = jnp.zeros((Bp, D_LRU), F32)
    zs = jnp.zeros((Bp, SSM_GROUPS, SSM_STATE), F32)
    zkv = jnp.zeros((Bp, WINDOW, D_KV), F32)

    xp, xs = x_prompt, x_sample
    st_p, st_s = [], []
    for l in range(depth):
        p = _layer_params(l, norm1, w_in, conv_w, conv_b, w_rg, b_rg, w_ig, b_ig, lru_lambda,
                          ssm_a_re, ssm_a_im, ssm_b_re, ssm_b_im, ssm_c_re, ssm_c_im, ssm_d, ssm_log_dt,
                          w_glu, b_glu, attn_sinks, w_out, norm2, w_up, w_down)
        final = l == depth - 1
        xp, sp = _layer(xp, p, tab_p, zconv, zh, zs, zs, zkv, zkv, gf, pos_base=0, final=final, CS=CHUNK)
        xs, ss = _layer(xs, p, tab_s, cache_conv_a[l], state_lru[l], state_ssm_re[l], state_ssm_im[l],
                        cache_k[l].reshape(Bs, WINDOW, D_KV), cache_v[l].reshape(Bs, WINDOW, D_KV), gf,
                        pos_base=PAST_LEN, final=final, CS=Ls)
        st_p.append(sp)
        st_s.append(ss)

    outs = [xp, xs]
    for sts in (st_p, st_s):
        for i in range(6):
            outs.append(jnp.stack([s[i] for s in sts], 0))
    return tuple(outs)
```

```python
import functools

import jax
import jax.numpy as jnp
from jax import lax
from jax.experimental import pallas as pl
from jax.experimental.pallas import tpu as pltpu

F32 = jnp.float32
BF16 = jnp.bfloat16

D_MODEL = 1024
PAST_LEN = 4096
CHUNK = 64
D_LRU = 256
CONV_W = 4
LRU_C = 8.0
D_SSM = 256
SSM_GROUPS = 16
SSM_STATE = 64
D_STATE = SSM_GROUPS * SSM_STATE
HEAD_DIM = 64
N_Q_HEADS = 8
N_KV_HEADS = 2
D_ATT = N_Q_HEADS * HEAD_DIM
D_KV = N_KV_HEADS * HEAD_DIM
WINDOW = 128
ROPE_THETA = 10000.0
D_AB = 2 * D_LRU + D_SSM
D_IN = D_AB + D_ATT + 2 * D_KV
D_FF = 4 * D_MODEL
RMS_EPS = 1e-6
NEG_INF = -1e30

LANES = 128
KEY_WIN = 2 * LANES
FF_CHUNK = 1024
SCAN_LANES = 256
SCAN_SUB = 16
VMEM_LIMIT = 56 * 1024 * 1024

T_ATTN = 256
ROWS_ATTN = 1024
PROJ_ROWS = 256
T_SCAN = 64
T_MLP = 512


def _rms(x, g):
    return (x * lax.rsqrt(jnp.mean(x * x, axis=-1, keepdims=True) + RMS_EPS)) * g


def _sigmoid(x):
    return 0.5 * jnp.tanh(0.5 * x) + 0.5


def _const_spec(shape):
    nd = len(shape)
    return pl.BlockSpec(shape, lambda *_: (0,) * nd, pipeline_mode=pl.Buffered(1))


def _inproj_attn_kernel(pos_base, T, CS, G, x_ref, g1_ref, win_ref, cos_ref, sin_ref, ck_ref, cv_ref, sink_ref,
                        pab_ref, oc_ref, kout_ref, vout_ref, kext_all, vext_all):
    tc = pl.program_id(1)

    @pl.when(tc == 0)
    def _():
        kext_all[...] = jnp.zeros(kext_all.shape, F32)
        vext_all[...] = jnp.zeros(vext_all.shape, F32)
        kext_all[:, 0:WINDOW, :] = ck_ref[...]
        vext_all[:, 0:WINDOW, :] = cv_ref[...]

    cos = cos_ref[...]
    sin = sin_ref[...]
    lane = lax.broadcasted_iota(jnp.int32, (1, LANES), 1)
    first_half = (lane % HEAD_DIM) < (HEAD_DIM // 2)
    low_head = lane < HEAD_DIM
    scale = HEAD_DIM ** -0.5
    n_out = kout_ref.shape[1]

    def rope(a):
        partner = jnp.where(first_half, pltpu.roll(a, LANES - HEAD_DIM // 2, 1), pltpu.roll(a, HEAD_DIM // 2, 1))
        return a * cos + partner * sin

    def variants(full):
        swapped = pltpu.roll(full, HEAD_DIM, 1)
        zero = jnp.zeros_like(full)
        return ((jnp.where(low_head, full, zero).astype(BF16), jnp.where(low_head, zero, swapped).astype(BF16)),
                (jnp.where(low_head, swapped, zero).astype(BF16), jnp.where(low_head, zero, full).astype(BF16)))

    PG = min(G, max(1, PROJ_ROWS // T))
    queries = []
    for g in range(G):
        kext, vext = kext_all.at[g], vext_all.at[g]
        if g % PG == 0:
            hn = _rms(x_ref[g:g + PG].reshape(PG * T, D_MODEL), g1_ref[...])
            proj_group = jnp.dot(hn.astype(BF16), win_ref[...], preferred_element_type=F32)
        proj = proj_group[(g % PG) * T:(g % PG + 1) * T]
        pab_ref[g] = proj[:, :D_AB]
        queries.append([(rope(proj[:, D_AB + j * LANES:D_AB + (j + 1) * LANES]) * scale).astype(BF16)
                        for j in range(D_ATT // LANES)])
        kext[WINDOW:WINDOW + T, :] = rope(proj[:, D_AB + D_ATT:D_AB + D_ATT + D_KV])
        vext[WINDOW:WINDOW + T, :] = proj[:, D_AB + D_ATT + D_KV:]
        kout_ref[g] = kext[WINDOW + T - n_out:WINDOW + T, :]
        vout_ref[g] = vext[WINDOW + T - n_out:WINDOW + T, :]

    kidx = lax.broadcasted_iota(jnp.int32, (1, KEY_WIN), 1)
    row = lax.broadcasted_iota(jnp.int32, (2 * CS, 1), 0)
    for g in range(G):
        kext, vext = kext_all.at[g], vext_all.at[g]
        qcols = queries[g]
        kvar = variants(kext[...])
        vvar = variants(vext[...])
        for c in range(T // CS):
            lo = c * CS
            pos0 = pos_base + tc * T + lo
            valid = (kidx >= WINDOW - pos0) & (kidx < WINDOW + CS)
            for kv in range(N_KV_HEADS):
                q2 = jnp.concatenate([qcols[2 * kv][lo:lo + CS], qcols[2 * kv + 1][lo:lo + CS]], axis=0)
                out = None
                for half in range(2):
                    kk = kvar[kv][half][lo:lo + KEY_WIN]
                    vv = vvar[kv][half][lo:lo + KEY_WIN]
                    s = lax.dot_general(q2, kk, (((1,), (1,)), ((), ())), preferred_element_type=F32)
                    s = jnp.where(valid, s, NEG_INF)
                    h0 = 4 * kv + half
                    sink = jnp.where(row < CS, sink_ref[h0], sink_ref[h0 + 2])
                    m = jnp.maximum(jnp.max(s, axis=-1, keepdims=True), sink)
                    e = jnp.exp(s - m)
                    denom = jnp.sum(e, axis=-1, keepdims=True) + jnp.exp(sink - m)
                    o = jnp.dot(e.astype(BF16), vv, preferred_element_type=F32) / denom
                    out = o if out is None else out + o
                oc_ref[g, lo:lo + CS, 2 * kv * LANES:(2 * kv + 1) * LANES] = out[:CS].astype(BF16)
                oc_ref[g, lo:lo + CS, (2 * kv + 1) * LANES:(2 * kv + 2) * LANES] = out[CS:].astype(BF16)
        kext[0:WINDOW, :] = kext[T:T + WINDOW, :]
        vext[0:WINDOW, :] = vext[T:T + WINDOW, :]


def _inproj_attn(x, g1, w_in, cos_t, sin_t, cache_k, cache_v, sinks, *, pos_base, T, CS):
    NB, L, _ = x.shape
    n_out = min(L, WINDOW)
    ext_rows = T - CS + KEY_WIN
    G = max(g for g in range(1, NB + 1) if NB % g == 0 and g * T <= max(ROWS_ATTN, T))
    kern = functools.partial(_inproj_attn_kernel, pos_base, T, CS, G)
    stream_block = lambda rows, width: pl.BlockSpec((G, rows, width), lambda b, t: (b, t, 0))
    stream_const = lambda rows, width: pl.BlockSpec((G, rows, width), lambda b, t: (b, 0, 0))
    return pl.pallas_call(
        kern,
        grid=(NB // G, L // T),
        in_specs=[
            stream_block(T, D_MODEL),
            _const_spec((1, D_MODEL)),
            _const_spec((D_MODEL, D_IN)),
            pl.BlockSpec((T, LANES), lambda b, t: (t, 0)),
            pl.BlockSpec((T, LANES), lambda b, t: (t, 0)),
            stream_const(WINDOW, D_KV),
            stream_const(WINDOW, D_KV),
            pl.BlockSpec(memory_space=pltpu.SMEM),
        ],
        out_specs=[
            stream_block(T, D_AB),
            stream_block(T, D_ATT),
            stream_const(n_out, D_KV),
            stream_const(n_out, D_KV),
        ],
        out_shape=[
            jax.ShapeDtypeStruct((NB, L, D_AB), F32),
            jax.ShapeDtypeStruct((NB, L, D_ATT), BF16),
            jax.ShapeDtypeStruct((NB, n_out, D_KV), F32),
            jax.ShapeDtypeStruct((NB, n_out, D_KV), F32),
        ],
        scratch_shapes=[pltpu.VMEM((G, ext_rows, D_KV), F32), pltpu.VMEM((G, ext_rows, D_KV), F32)],
        compiler_params=pltpu.CompilerParams(dimension_semantics=("arbitrary", "arbitrary"),
                                             vmem_limit_bytes=VMEM_LIMIT),
        name="inproj_attn",
    )(x, g1, w_in, cos_t, sin_t, cache_k, cache_v, sinks)


def _lru_s5_kernel(NB, T, TS, pab_ref, cbuf0_ref, h0_ref, sr0_ref, si0_ref,
                   convw_ref, convb_ref, wgate_ref, bgate_ref, lamc_ref,
                   abar_ref, bb_ref, cc_ref, d_ref, wglu_ref, bglu_ref,
                   oab_ref, cbuf_out, h_out, sr_out, si_out,
                   uext, gu_s, a_s, b_s, bu_s, h_st, s_st):
    tc = pl.program_id(0)
    R = T * NB
    RS = TS * NB
    TAIL = (CONV_W - 1) * NB

    @pl.when(tc == 0)
    def _():
        uext[0:TAIL, :] = cbuf0_ref[...]
        h_st[...] = h0_ref[...]
        s_st[:, 0:D_STATE] = sr0_ref[...]
        s_st[:, D_STATE:] = si0_ref[...]

    def to_time_major(t0, c0):
        return jnp.swapaxes(pab_ref[:, t0:t0 + TS, c0:c0 + D_LRU], 0, 1).reshape(RS, D_LRU)

    for sb in range(T // TS):
        r0 = sb * RS
        uext[TAIL + r0:TAIL + r0 + RS, :] = to_time_major(sb * TS, 0)
        gu_s[r0:r0 + RS, 0:D_LRU] = to_time_major(sb * TS, D_LRU)
        gu_s[r0:r0 + RS, D_LRU:] = to_time_major(sb * TS, 2 * D_LRU)

    for sb in range(T // TS):
        r0 = sb * RS
        xc = convb_ref[...]
        for j in range(CONV_W):
            xc = xc + uext[j * NB + r0:j * NB + r0 + RS, :] * convw_ref[j:j + 1, :]
        g = jnp.dot(xc.astype(BF16), wgate_ref[...], preferred_element_type=F32) + bgate_ref[...]
        r_gate = _sigmoid(g[:, :D_LRU])
        i_gate = _sigmoid(g[:, D_LRU:])
        log_a = r_gate * lamc_ref[...]
        a_s[r0:r0 + RS, :] = jnp.exp(log_a)
        b_s[r0:r0 + RS, :] = jnp.sqrt(1.0 - jnp.exp(2.0 * log_a)) * (i_gate * xc)
        bu_s[r0:r0 + RS, :] = jnp.dot(gu_s[r0:r0 + RS, D_LRU:].astype(BF16), bb_ref[...],
                                      preferred_element_type=F32)

    tail = uext[R:R + TAIL, :]
    uext[0:TAIL, :] = tail
    cbuf_out[...] = tail

    def lru_step(t, h):
        r0 = pl.multiple_of(t * NB, NB)
        h = a_s[pl.ds(r0, NB), :] * h + b_s[pl.ds(r0, NB), :]
        b_s[pl.ds(r0, NB), :] = h
        return h

    h_last = lax.fori_loop(0, T, lru_step, h_st[...], unroll=True)
    h_st[...] = h_last
    h_out[...] = h_last

    for ch in range(D_STATE // SCAN_LANES):
        c0 = ch * SCAN_LANES
        ar = jnp.broadcast_to(abar_ref[0:1, c0:c0 + SCAN_LANES], (NB, SCAN_LANES))
        ai = jnp.broadcast_to(abar_ref[1:2, c0:c0 + SCAN_LANES], (NB, SCAN_LANES))

        def s5_step(t, carry, c0=c0, ar=ar, ai=ai):
            sr, si = carry
            r0 = pl.multiple_of(t * NB, NB)
            nr = ar * sr - ai * si + bu_s[pl.ds(r0, NB), c0:c0 + SCAN_LANES]
            ni = ar * si + ai * sr + bu_s[pl.ds(r0, NB), D_STATE + c0:D_STATE + c0 + SCAN_LANES]
            bu_s[pl.ds(r0, NB), c0:c0 + SCAN_LANES] = nr
            bu_s[pl.ds(r0, NB), D_STATE + c0:D_STATE + c0 + SCAN_LANES] = ni
            return nr, ni

        sr, si = lax.fori_loop(0, T, s5_step,
                               (s_st[:, c0:c0 + SCAN_LANES], s_st[:, D_STATE + c0:D_STATE + c0 + SCAN_LANES]),
                               unroll=True)
        s_st[:, c0:c0 + SCAN_LANES] = sr
        s_st[:, D_STATE + c0:D_STATE + c0 + SCAN_LANES] = si

    sr_out[...] = s_st[:, 0:D_STATE]
    si_out[...] = s_st[:, D_STATE:]

    def to_batch_major(v):
        return jnp.swapaxes(v.reshape(TS, NB, D_LRU), 0, 1).astype(BF16)

    for sb in range(T // TS):
        r0 = sb * RS
        t0 = sb * TS
        out_a = b_s[r0:r0 + RS, :] * jax.nn.gelu(gu_s[r0:r0 + RS, 0:D_LRU])
        oab_ref[:, t0:t0 + TS, 0:D_LRU] = to_batch_major(out_a)
        y = jnp.dot(bu_s[r0:r0 + RS, :].astype(BF16), cc_ref[...], preferred_element_type=F32)
        y = y + d_ref[...] * gu_s[r0:r0 + RS, D_LRU:]
        z = jax.nn.gelu(y)
        gl = jnp.dot(z.astype(BF16), wglu_ref[...], preferred_element_type=F32) + bglu_ref[...]
        oab_ref[:, t0:t0 + TS, D_LRU:] = to_batch_major(z * _sigmoid(gl))


def _lru_s5(pab, cbuf0, h0, sr0, si0, convw, convb, wgate, bgate, lamc, abar, bb, cc, d, wglu, bglu, *, T):
    NB, L, _ = pab.shape
    R = T * NB
    TAIL = (CONV_W - 1) * NB
    TS = min(T, SCAN_SUB)
    kern = functools.partial(_lru_s5_kernel, NB, T, TS)
    consts = [cbuf0, h0, sr0, si0, convw, convb, wgate, bgate, lamc, abar, bb, cc, d, wglu, bglu]
    return pl.pallas_call(
        kern,
        grid=(L // T,),
        in_specs=[pl.BlockSpec((NB, T, D_AB), lambda t: (0, t, 0))] + [_const_spec(c.shape) for c in consts],
        out_specs=[
            pl.BlockSpec((NB, T, 2 * D_LRU), lambda t: (0, t, 0)),
            pl.BlockSpec((TAIL, D_LRU), lambda t: (0, 0)),
            pl.BlockSpec((NB, D_LRU), lambda t: (0, 0)),
            pl.BlockSpec((NB, D_STATE), lambda t: (0, 0)),
            pl.BlockSpec((NB, D_STATE), lambda t: (0, 0)),
        ],
        out_shape=[
            jax.ShapeDtypeStruct((NB, L, 2 * D_LRU), BF16),
            jax.ShapeDtypeStruct((TAIL, D_LRU), F32),
            jax.ShapeDtypeStruct((NB, D_LRU), F32),
            jax.ShapeDtypeStruct((NB, D_STATE), F32),
            jax.ShapeDtypeStruct((NB, D_STATE), F32),
        ],
        scratch_shapes=[
            pltpu.VMEM((R + TAIL, D_LRU), F32),
            pltpu.VMEM((R, 2 * D_LRU), F32),
            pltpu.VMEM((R, D_LRU), F32),
            pltpu.VMEM((R, D_LRU), F32),
            pltpu.VMEM((R, 2 * D_STATE), F32),
            pltpu.VMEM((NB, D_LRU), F32),
            pltpu.VMEM((NB, 2 * D_STATE), F32),
        ],
        compiler_params=pltpu.CompilerParams(dimension_semantics=("arbitrary",), vmem_limit_bytes=VMEM_LIMIT),
        name="lru_s5",
    )(pab, *consts)


def _outproj_mlp_kernel(final, x_ref, oab_ref, oc_ref, wout_ref, g2_ref, wup_ref, wdn_ref, gf_ref, y_ref):
    x = x_ref[0]
    mix = jnp.dot(oab_ref[0], wout_ref[0:2 * D_LRU, :], preferred_element_type=F32)
    mix = mix + jnp.dot(oc_ref[0], wout_ref[2 * D_LRU:, :], preferred_element_type=F32)
    x = x + mix
    hm = _rms(x, g2_ref[...]).astype(BF16)
    down = None
    for c in range(D_FF // FF_CHUNK):
        up = jnp.dot(hm, wup_ref[:, c * FF_CHUNK:(c + 1) * FF_CHUNK], preferred_element_type=F32)
        act = jnp.square(jnp.maximum(up, 0.0)).astype(BF16)
        part = jnp.dot(act, wdn_ref[c * FF_CHUNK:(c + 1) * FF_CHUNK, :], preferred_element_type=F32)
        down = part if down is None else down + part
    x = x + down
    if final:
        x = _rms(x, gf_ref[...])
    y_ref[0] = x


def _outproj_mlp(x, oab, oc, w_out, g2, w_up, w_down, gf, *, final, T):
    NB, L, _ = x.shape
    kern = functools.partial(_outproj_mlp_kernel, final)
    row_block = lambda width: pl.BlockSpec((1, T, width), lambda b, t: (b, t, 0))
    return pl.pallas_call(
        kern,
        grid=(NB, L // T),
        in_specs=[
            row_block(D_MODEL), row_block(2 * D_LRU), row_block(D_ATT),
            _const_spec((D_MODEL, D_MODEL)),
            _const_spec((1, D_MODEL)),
            _const_spec((D_MODEL, D_FF)),
            _const_spec((D_FF, D_MODEL)),
            _const_spec((1, D_MODEL)),
        ],
        out_specs=row_block(D_MODEL),
        out_shape=jax.ShapeDtypeStruct((NB, L, D_MODEL), F32),
        compiler_params=pltpu.CompilerParams(dimension_semantics=("arbitrary", "arbitrary"),
                                             vmem_limit_bytes=VMEM_LIMIT),
        name="outproj_mlp",
    )(x, oab, oc, w_out, g2, w_up, w_down, gf)


def _block_diag(blocks):
    n, r, c = blocks.shape
    on_diag = jnp.eye(n, dtype=bool)[:, None, :, None]
    return jnp.where(on_diag, blocks[:, :, None, :], jnp.zeros((), blocks.dtype)).reshape(n * r, n * c)


def _rope_tables(pos_base, L):
    half = HEAD_DIM // 2
    inv = ROPE_THETA ** (-jnp.arange(half, dtype=F32) / half)
    ang = (pos_base + jnp.arange(L, dtype=jnp.int32)).astype(F32)[:, None] * inv[None, :]
    cos, sin = jnp.cos(ang), jnp.sin(ang)
    reps = LANES // HEAD_DIM
    return (jnp.tile(jnp.concatenate([cos, cos], axis=-1), (1, reps)),
            jnp.tile(jnp.concatenate([-sin, sin], axis=-1), (1, reps)))


def _layer_params(l, norm1, w_in, conv_w, conv_b, w_rg, b_rg, w_ig, b_ig, lru_lambda,
                  ssm_a_re, ssm_a_im, ssm_b_re, ssm_b_im, ssm_c_re, ssm_c_im, ssm_d, ssm_log_dt,
                  w_glu, b_glu, attn_sinks, w_out, norm2, w_up, w_down):
    dt = jnp.exp(ssm_log_dt[l])[:, None]
    lr, li = ssm_a_re[l], ssm_a_im[l]
    mag = jnp.exp(lr * dt)
    abar_r, abar_i = mag * jnp.cos(li * dt), mag * jnp.sin(li * dt)
    den = lr * lr + li * li
    nr = abar_r - 1.0
    fr = (nr * lr + abar_i * li) / den
    fi = (abar_i * lr - nr * li) / den
    br, bi = ssm_b_re[l], ssm_b_im[l]
    bb_r = fr[..., None] * br - fi[..., None] * bi
    bb_i = fr[..., None] * bi + fi[..., None] * br
    bb = jnp.concatenate([_block_diag(jnp.swapaxes(bb_r, 1, 2)), _block_diag(jnp.swapaxes(bb_i, 1, 2))], axis=1)
    cc = jnp.concatenate([_block_diag(jnp.swapaxes(ssm_c_re[l], 1, 2)),
                          -_block_diag(jnp.swapaxes(ssm_c_im[l], 1, 2))], axis=0)
    return dict(
        g1=norm1[l][None, :], w_in=w_in[l].astype(BF16),
        convw=conv_w[l], convb=conv_b[l][None, :],
        wgate=jnp.concatenate([_block_diag(w_rg[l]), _block_diag(w_ig[l])], axis=1).astype(BF16),
        bgate=jnp.concatenate([b_rg[l], b_ig[l]])[None, :],
        lamc=(-LRU_C * jax.nn.softplus(-lru_lambda[l]))[None, :],
        abar=jnp.stack([abar_r.reshape(-1), abar_i.reshape(-1)], axis=0),
        bb=bb.astype(BF16), cc=cc.astype(BF16), d=ssm_d[l][None, :],
        wglu=w_glu[l].astype(BF16), bglu=b_glu[l][None, :],
        sinks=attn_sinks[l], w_out=w_out[l].astype(BF16), g2=norm2[l][None, :],
        w_up=w_up[l].astype(BF16), w_down=w_down[l].astype(BF16),
    )


def _layer(x, p, tables, conv_buf, h0, s_re, s_im, cache_k, cache_v, gf, *, pos_base, final, CS):
    NB, L, _ = x.shape
    pab, oc, k_new, v_new = _inproj_attn(x, p['g1'], p['w_in'], tables[0], tables[1], cache_k, cache_v, p['sinks'],
                                         pos_base=pos_base, T=min(T_ATTN, L), CS=CS)
    cbuf0 = jnp.swapaxes(conv_buf, 0, 1).reshape((CONV_W - 1) * NB, D_LRU)
    oab, cbuf, h_last, sr, si = _lru_s5(
        pab, cbuf0, h0, s_re.reshape(NB, D_STATE), s_im.reshape(NB, D_STATE),
        p['convw'], p['convb'], p['wgate'], p['bgate'], p['lamc'], p['abar'], p['bb'], p['cc'], p['d'],
        p['wglu'], p['bglu'], T=min(T_SCAN, L))
    if NB * L <= T_MLP:
        fold = lambda a: a.reshape(1, NB * L, a.shape[-1])
        x_new = _outproj_mlp(fold(x), fold(oab), fold(oc), p['w_out'], p['g2'], p['w_up'], p['w_down'], gf,
                             final=final, T=NB * L).reshape(NB, L, D_MODEL)
    else:
        x_new = _outproj_mlp(x, oab, oc, p['w_out'], p['g2'], p['w_up'], p['w_down'], gf,
                             final=final, T=min(T_MLP, L))
    n_out = k_new.shape[1]
    state = (jnp.swapaxes(cbuf.reshape(CONV_W - 1, NB, D_LRU), 0, 1), h_last,
             sr.reshape(NB, SSM_GROUPS, SSM_STATE), si.reshape(NB, SSM_GROUPS, SSM_STATE),
             k_new.reshape(NB, n_out, N_KV_HEADS, HEAD_DIM), v_new.reshape(NB, n_out, N_KV_HEADS, HEAD_DIM))
    return x_new, state


def kernel(x_prompt, x_sample, cache_conv_a, state_lru, state_ssm_re, state_ssm_im, cache_k, cache_v, norm1, w_in, conv_w, conv_b, w_rg, b_rg, w_ig, b_ig, lru_lambda, ssm_a_re, ssm_a_im, ssm_b_re, ssm_b_im, ssm_c_re, ssm_c_im, ssm_d, ssm_log_dt, w_glu, b_glu, attn_sinks, w_out, norm2, w_up, w_down, norm_f):
    Bp, Lp = x_prompt.shape[0], x_prompt.shape[1]
    Bs, Ls = x_sample.shape[0], x_sample.shape[1]
    depth = w_in.shape[0]
    tab_p = _rope_tables(0, Lp)
    tab_s = _rope_tables(PAST_LEN, Ls)
    gf = norm_f[None, :]
    zconv = jnp.zeros((Bp, CONV_W - 1, D_LRU), F32)
    zh = jnp.zeros((Bp, D_LRU), F32)
    zs = jnp.zeros((Bp, SSM_GROUPS, SSM_STATE), F32)
    zkv = jnp.zeros((Bp, WINDOW, D_KV), F32)

    xp, xs = x_prompt, x_sample
    st_p, st_s = [], []
    for l in range(depth):
        p = _layer_params(l, norm1, w_in, conv_w, conv_b, w_rg, b_rg, w_ig, b_ig, lru_lambda,
                          ssm_a_re, ssm_a_im, ssm_b_re, ssm_b_im, ssm_c_re, ssm_c_im, ssm_d, ssm_log_dt,
                          w_glu, b_glu, attn_sinks, w_out, norm2, w_up, w_down)
        final = l == depth - 1
        xp, sp = _layer(xp, p, tab_p, zconv, zh, zs, zs, zkv, zkv, gf, pos_base=0, final=final, CS=CHUNK)
        xs, ss = _layer(xs, p, tab_s, cache_conv_a[l], state_lru[l], state_ssm_re[l], state_ssm_im[l],
                        cache_k[l].reshape(Bs, WINDOW, D_KV), cache_v[l].reshape(Bs, WINDOW, D_KV), gf,
                        pos_base=PAST_LEN, final=final, CS=Ls)
        st_p.append(sp)
        st_s.append(ss)

    outs = [xp, xs]
    for sts in (st_p, st_s):
        for i in range(6):
            outs.append(jnp.stack([s[i] for s in sts], 0))
    return tuple(outs)
```

```python
import functools

import jax
import jax.numpy as jnp
from jax import lax
from jax.experimental import pallas as pl
from jax.experimental.pallas import tpu as pltpu

F32 = jnp.float32
BF16 = jnp.bfloat16

D_MODEL = 1024
PAST_LEN = 4096
CHUNK = 64
D_LRU = 256
CONV_W = 4
LRU_C = 8.0
D_SSM = 256
SSM_GROUPS = 16
SSM_STATE = 64
D_STATE = SSM_GROUPS * SSM_STATE
HEAD_DIM = 64
N_Q_HEADS = 8
N_KV_HEADS = 2
D_ATT = N_Q_HEADS * HEAD_DIM
D_KV = N_KV_HEADS * HEAD_DIM
WINDOW = 128
ROPE_THETA = 10000.0
D_AB = 2 * D_LRU + D_SSM
D_IN = D_AB + D_ATT + 2 * D_KV
D_FF = 4 * D_MODEL
RMS_EPS = 1e-6
NEG_INF = -1e30
LOG2E = 1.4426950408889634

LANES = 128
KEY_WIN = 2 * LANES
FF_CHUNK = 1024
SCAN_LANES = 256
SCAN_SUB = 128
VMEM_LIMIT = 56 * 1024 * 1024
VMEM_LIMIT_MLP = 60 * 1024 * 1024

T_ATTN = 256
ROWS_ATTN = 1024
PROJ_ROWS = 256
T_SCAN = 128
T_MLP = 1024


def _rms(x, g):
    return (x * lax.rsqrt(jnp.mean(x * x, axis=-1, keepdims=True) + RMS_EPS)) * g


def _sigmoid(x):
    return 0.5 * jnp.tanh(0.5 * x) + 0.5


def _const_spec(shape):
    nd = len(shape)
    return pl.BlockSpec(shape, lambda *_: (0,) * nd, pipeline_mode=pl.Buffered(1))


def _layer_spec(stacked, layer):
    nd = stacked.ndim - 1
    return pl.BlockSpec((None,) + stacked.shape[1:], lambda *_: (layer,) + (0,) * nd, pipeline_mode=pl.Buffered(1))


def _inproj_attn_kernel(pos_base, T, CS, G, layer, x_ref, g1_ref, win_ref, cos_ref, sin_ref, ck_ref, cv_ref, sink_ref,
                        pab_ref, oc_ref, kout_ref, vout_ref, kext_all, vext_all):
    tc = pl.program_id(1)

    @pl.when(tc == 0)
    def _():
        kext_all[...] = jnp.zeros(kext_all.shape, F32)
        vext_all[...] = jnp.zeros(vext_all.shape, F32)
        kext_all[:, 0:WINDOW, :] = ck_ref[...]
        vext_all[:, 0:WINDOW, :] = cv_ref[...]

    cos = cos_ref[...]
    sin = sin_ref[...]
    lane = lax.broadcasted_iota(jnp.int32, (1, LANES), 1)
    first_half = (lane % HEAD_DIM) < (HEAD_DIM // 2)
    low_head = lane < HEAD_DIM
    scale = HEAD_DIM ** -0.5 * LOG2E
    n_out = kout_ref.shape[1]

    def rope(a):
        partner = jnp.where(first_half, pltpu.roll(a, LANES - HEAD_DIM // 2, 1), pltpu.roll(a, HEAD_DIM // 2, 1))
        return a * cos + partner * sin

    def variants(full):
        swapped = pltpu.roll(full, HEAD_DIM, 1)
        zero = jnp.zeros_like(full)
        return ((jnp.where(low_head, full, zero).astype(BF16), jnp.where(low_head, zero, swapped).astype(BF16)),
                (jnp.where(low_head, swapped, zero).astype(BF16), jnp.where(low_head, zero, full).astype(BF16)))

    PG = min(G, max(1, PROJ_ROWS // T))
    queries = []
    for g in range(G):
        kext, vext = kext_all.at[g], vext_all.at[g]
        if g % PG == 0:
            hn = _rms(x_ref[g:g + PG].reshape(PG * T, D_MODEL), g1_ref[...])
            proj_group = jnp.dot(hn.astype(BF16), win_ref[...], preferred_element_type=F32)
        proj = proj_group[(g % PG) * T:(g % PG + 1) * T]
        pab_ref[g] = proj[:, :D_AB]
        queries.append([(rope(proj[:, D_AB + j * LANES:D_AB + (j + 1) * LANES]) * scale).astype(BF16)
                        for j in range(D_ATT // LANES)])
        kext[WINDOW:WINDOW + T, :] = rope(proj[:, D_AB + D_ATT:D_AB + D_ATT + D_KV])
        vext[WINDOW:WINDOW + T, :] = proj[:, D_AB + D_ATT + D_KV:]
        kout_ref[g] = kext[WINDOW + T - n_out:WINDOW + T, :]
        vout_ref[g] = vext[WINDOW + T - n_out:WINDOW + T, :]

    kidx = lax.broadcasted_iota(jnp.int32, (1, KEY_WIN), 1)
    row = lax.broadcasted_iota(jnp.int32, (2 * CS, 1), 0)
    for g in range(G):
        kext, vext = kext_all.at[g], vext_all.at[g]
        qcols = queries[g]
        kvar = variants(kext[...])
        vvar = variants(vext[...])
        for c in range(T // CS):
            lo = c * CS
            pos0 = pos_base + tc * T + lo
            valid = (kidx >= WINDOW - pos0) & (kidx < WINDOW + CS)
            for kv in range(N_KV_HEADS):
                q2 = jnp.concatenate([qcols[2 * kv][lo:lo + CS], qcols[2 * kv + 1][lo:lo + CS]], axis=0)
                out = None
                for half in range(2):
                    kk = kvar[kv][half][lo:lo + KEY_WIN]
                    vv = vvar[kv][half][lo:lo + KEY_WIN]
                    s = lax.dot_general(q2, kk, (((1,), (1,)), ((), ())), preferred_element_type=F32)
                    s = jnp.where(valid, s, NEG_INF)
                    h0 = 4 * kv + half
                    sink = jnp.where(row < CS, sink_ref[layer, h0], sink_ref[layer, h0 + 2]) * LOG2E
                    m = jnp.maximum(jnp.max(s, axis=-1, keepdims=True), sink)
                    e = jnp.exp2(s - m)
                    denom = jnp.sum(e, axis=-1, keepdims=True) + jnp.exp2(sink - m)
                    o = jnp.dot(e.astype(BF16), vv, preferred_element_type=F32) / denom
                    out = o if out is None else out + o
                oc_ref[g, lo:lo + CS, 2 * kv * LANES:(2 * kv + 1) * LANES] = out[:CS].astype(BF16)
                oc_ref[g, lo:lo + CS, (2 * kv + 1) * LANES:(2 * kv + 2) * LANES] = out[CS:].astype(BF16)
        kext[0:WINDOW, :] = kext[T:T + WINDOW, :]
        vext[0:WINDOW, :] = vext[T:T + WINDOW, :]


def _inproj_attn(x, p, layer, cos_t, sin_t, cache_k, cache_v, *, pos_base, T, CS):
    NB, L, _ = x.shape
    n_out = min(L, WINDOW)
    ext_rows = T - CS + KEY_WIN
    G = max(g for g in range(1, NB + 1) if NB % g == 0 and g * T <= max(ROWS_ATTN, T))
    kern = functools.partial(_inproj_attn_kernel, pos_base, T, CS, G, layer)
    stream_block = lambda rows, width: pl.BlockSpec((G, rows, width), lambda b, t: (b, t, 0))
    stream_const = lambda rows, width: pl.BlockSpec((G, rows, width), lambda b, t: (b, 0, 0))
    return pl.pallas_call(
        kern,
        grid=(NB // G, L // T),
        in_specs=[
            stream_block(T, D_MODEL),
            _layer_spec(p['g1'], layer),
            _layer_spec(p['w_in'], layer),
            pl.BlockSpec((T, LANES), lambda b, t: (t, 0)),
            pl.BlockSpec((T, LANES), lambda b, t: (t, 0)),
            stream_const(WINDOW, D_KV),
            stream_const(WINDOW, D_KV),
            pl.BlockSpec(memory_space=pltpu.SMEM),
        ],
        out_specs=[
            stream_block(T, D_AB),
            stream_block(T, D_ATT),
            stream_const(n_out, D_KV),
            stream_const(n_out, D_KV),
        ],
        out_shape=[
            jax.ShapeDtypeStruct((NB, L, D_AB), F32),
            jax.ShapeDtypeStruct((NB, L, D_ATT), BF16),
            jax.ShapeDtypeStruct((NB, n_out, D_KV), F32),
            jax.ShapeDtypeStruct((NB, n_out, D_KV), F32),
        ],
        scratch_shapes=[pltpu.VMEM((G, ext_rows, D_KV), F32), pltpu.VMEM((G, ext_rows, D_KV), F32)],
        compiler_params=pltpu.CompilerParams(dimension_semantics=("arbitrary", "arbitrary"),
                                             vmem_limit_bytes=VMEM_LIMIT),
        name="inproj_attn",
    )(x, p['g1'], p['w_in'], cos_t, sin_t, cache_k, cache_v, p['sinks'])


def _lru_s5_kernel(NB, T, TS, pab_ref, cbuf0_ref, h0_ref, sr0_ref, si0_ref,
                   convw_ref, convb_ref, wgate_ref, bgate_ref, lamc_ref,
                   abar_ref, bb_ref, cc_ref, d_ref, wglu_ref, bglu_ref,
                   oab_ref, cbuf_out, h_out, sr_out, si_out,
                   uext, gu_s, a_s, b_s, bu_s, h_st, s_st):
    tc = pl.program_id(0)
    R = T * NB
    RS = TS * NB
    TAIL = (CONV_W - 1) * NB

    @pl.when(tc == 0)
    def _():
        uext[0:TAIL, :] = cbuf0_ref[...]
        h_st[...] = h0_ref[...]
        s_st[:, 0:D_STATE] = sr0_ref[...]
        s_st[:, D_STATE:] = si0_ref[...]

    def to_time_major(t0, c0):
        return jnp.swapaxes(pab_ref[:, t0:t0 + TS, c0:c0 + D_LRU], 0, 1).reshape(RS, D_LRU)

    for sb in range(T // TS):
        r0 = sb * RS
        uext[TAIL + r0:TAIL + r0 + RS, :] = to_time_major(sb * TS, 0)
        gu_s[r0:r0 + RS, 0:D_LRU] = to_time_major(sb * TS, D_LRU)
        gu_s[r0:r0 + RS, D_LRU:] = to_time_major(sb * TS, 2 * D_LRU)

    for sb in range(T // TS):
        r0 = sb * RS
        xc = convb_ref[...]
        for j in range(CONV_W):
            xc = xc + uext[j * NB + r0:j * NB + r0 + RS, :] * convw_ref[j:j + 1, :]
        g = jnp.dot(xc.astype(BF16), wgate_ref[...], preferred_element_type=F32) + bgate_ref[...]
        r_gate = _sigmoid(g[:, :D_LRU])
        i_gate = _sigmoid(g[:, D_LRU:])
        log_a = r_gate * lamc_ref[...]
        a_s[r0:r0 + RS, :] = jnp.exp(log_a)
        b_s[r0:r0 + RS, :] = jnp.sqrt(1.0 - jnp.exp(2.0 * log_a)) * (i_gate * xc)
        bu_s[r0:r0 + RS, :] = jnp.dot(gu_s[r0:r0 + RS, D_LRU:].astype(BF16), bb_ref[...],
                                      preferred_element_type=F32)

    tail = uext[R:R + TAIL, :]
    uext[0:TAIL, :] = tail
    cbuf_out[...] = tail

    def lru_step(t, h):
        r0 = pl.multiple_of(t * NB, NB)
        h = a_s[pl.ds(r0, NB), :] * h + b_s[pl.ds(r0, NB), :]
        b_s[pl.ds(r0, NB), :] = h
        return h

    h_last = lax.fori_loop(0, T, lru_step, h_st[...], unroll=True)
    h_st[...] = h_last
    h_out[...] = h_last

    for ch in range(D_STATE // SCAN_LANES):
        c0 = ch * SCAN_LANES
        ar = jnp.broadcast_to(abar_ref[0:1, c0:c0 + SCAN_LANES], (NB, SCAN_LANES))
        ai = jnp.broadcast_to(abar_ref[1:2, c0:c0 + SCAN_LANES], (NB, SCAN_LANES))

        def s5_step(t, carry, c0=c0, ar=ar, ai=ai):
            sr, si = carry
            r0 = pl.multiple_of(t * NB, NB)
            nr = ar * sr - ai * si + bu_s[pl.ds(r0, NB), c0:c0 + SCAN_LANES]
            ni = ar * si + ai * sr + bu_s[pl.ds(r0, NB), D_STATE + c0:D_STATE + c0 + SCAN_LANES]
            bu_s[pl.ds(r0, NB), c0:c0 + SCAN_LANES] = nr
            bu_s[pl.ds(r0, NB), D_STATE + c0:D_STATE + c0 + SCAN_LANES] = ni
            return nr, ni

        sr, si = lax.fori_loop(0, T, s5_step,
                               (s_st[:, c0:c0 + SCAN_LANES], s_st[:, D_STATE + c0:D_STATE + c0 + SCAN_LANES]),
                               unroll=True)
        s_st[:, c0:c0 + SCAN_LANES] = sr
        s_st[:, D_STATE + c0:D_STATE + c0 + SCAN_LANES] = si

    sr_out[...] = s_st[:, 0:D_STATE]
    si_out[...] = s_st[:, D_STATE:]

    def to_batch_major(v):
        return jnp.swapaxes(v.reshape(TS, NB, D_LRU), 0, 1).astype(BF16)

    for sb in range(T // TS):
        r0 = sb * RS
        t0 = sb * TS
        out_a = b_s[r0:r0 + RS, :] * jax.nn.gelu(gu_s[r0:r0 + RS, 0:D_LRU])
        oab_ref[:, t0:t0 + TS, 0:D_LRU] = to_batch_major(out_a)
        y = jnp.dot(bu_s[r0:r0 + RS, :].astype(BF16), cc_ref[...], preferred_element_type=F32)
        y = y + d_ref[...] * gu_s[r0:r0 + RS, D_LRU:]
        z = jax.nn.gelu(y)
        gl = jnp.dot(z.astype(BF16), wglu_ref[...], preferred_element_type=F32) + bglu_ref[...]
        oab_ref[:, t0:t0 + TS, D_LRU:] = to_batch_major(z * _sigmoid(gl))


def _lru_s5(pab, cbuf0, h0, sr0, si0, p, layer, *, T):
    NB, L, _ = pab.shape
    R = T * NB
    TAIL = (CONV_W - 1) * NB
    TS = min(T, SCAN_SUB)
    kern = functools.partial(_lru_s5_kernel, NB, T, TS)
    states = [cbuf0, h0, sr0, si0]
    params = [p[k] for k in ('convw', 'convb', 'wgate', 'bgate', 'lamc', 'abar', 'bb', 'cc', 'd', 'wglu', 'bglu')]
    return pl.pallas_call(
        kern,
        grid=(L // T,),
        in_specs=([pl.BlockSpec((NB, T, D_AB), lambda t: (0, t, 0))] + [_const_spec(c.shape) for c in states]
                  + [_layer_spec(c, layer) for c in params]),
        out_specs=[
            pl.BlockSpec((NB, T, 2 * D_LRU), lambda t: (0, t, 0)),
            pl.BlockSpec((TAIL, D_LRU), lambda t: (0, 0)),
            pl.BlockSpec((NB, D_LRU), lambda t: (0, 0)),
            pl.BlockSpec((NB, D_STATE), lambda t: (0, 0)),
            pl.BlockSpec((NB, D_STATE), lambda t: (0, 0)),
        ],
        out_shape=[
            jax.ShapeDtypeStruct((NB, L, 2 * D_LRU), BF16),
            jax.ShapeDtypeStruct((TAIL, D_LRU), F32),
            jax.ShapeDtypeStruct((NB, D_LRU), F32),
            jax.ShapeDtypeStruct((NB, D_STATE), F32),
            jax.ShapeDtypeStruct((NB, D_STATE), F32),
        ],
        scratch_shapes=[
            pltpu.VMEM((R + TAIL, D_LRU), F32),
            pltpu.VMEM((R, 2 * D_LRU), F32),
            pltpu.VMEM((R, D_LRU), F32),
            pltpu.VMEM((R, D_LRU), F32),
            pltpu.VMEM((R, 2 * D_STATE), F32),
            pltpu.VMEM((NB, D_LRU), F32),
            pltpu.VMEM((NB, 2 * D_STATE), F32),
        ],
        compiler_params=pltpu.CompilerParams(dimension_semantics=("arbitrary",), vmem_limit_bytes=VMEM_LIMIT),
        name="lru_s5",
    )(pab, *states, *params)


def _outproj_mlp_kernel(final, x_ref, oab_ref, oc_ref, wout_ref, g2_ref, wup_ref, wdn_ref, gf_ref, y_ref):
    x = x_ref[0]
    mix = jnp.dot(oab_ref[0], wout_ref[0:2 * D_LRU, :], preferred_element_type=F32)
    mix = mix + jnp.dot(oc_ref[0], wout_ref[2 * D_LRU:, :], preferred_element_type=F32)
    x = x + mix
    hm = _rms(x, g2_ref[...]).astype(BF16)
    down = None
    for c in range(D_FF // FF_CHUNK):
        up = jnp.dot(hm, wup_ref[:, c * FF_CHUNK:(c + 1) * FF_CHUNK], preferred_element_type=F32)
        act = jnp.square(jnp.maximum(up, 0.0)).astype(BF16)
        part = jnp.dot(act, wdn_ref[c * FF_CHUNK:(c + 1) * FF_CHUNK, :], preferred_element_type=F32)
        down = part if down is None else down + part
    x = x + down
    if final:
        x = _rms(x, gf_ref[...])
    y_ref[0] = x


def _outproj_mlp(x, oab, oc, p, layer, gf, *, final, T):
    NB, L, _ = x.shape
    kern = functools.partial(_outproj_mlp_kernel, final)
    row_block = lambda width: pl.BlockSpec((1, T, width), lambda b, t: (b, t, 0))
    return pl.pallas_call(
        kern,
        grid=(NB, L // T),
        in_specs=[
            row_block(D_MODEL), row_block(2 * D_LRU), row_block(D_ATT),
            _layer_spec(p['w_out'], layer), _layer_spec(p['g2'], layer),
            _layer_spec(p['w_up'], layer), _layer_spec(p['w_down'], layer),
            _const_spec((1, D_MODEL)),
        ],
        out_specs=row_block(D_MODEL),
        out_shape=jax.ShapeDtypeStruct((NB, L, D_MODEL), F32),
        compiler_params=pltpu.CompilerParams(dimension_semantics=("arbitrary", "arbitrary"),
                                             vmem_limit_bytes=VMEM_LIMIT_MLP),
        name="outproj_mlp",
    )(x, oab, oc, p['w_out'], p['g2'], p['w_up'], p['w_down'], gf)


def _block_diag(blocks):
    n, r, c = blocks.shape[-3:]
    on_diag = jnp.eye(n, dtype=bool)[:, None, :, None]
    full = jnp.where(on_diag, blocks[..., :, :, None, :], jnp.zeros((), blocks.dtype))
    return full.reshape(blocks.shape[:-3] + (n * r, n * c))


def _rope_tables(pos_base, L):
    half = HEAD_DIM // 2
    inv = ROPE_THETA ** (-jnp.arange(half, dtype=F32) / half)
    ang = (pos_base + jnp.arange(L, dtype=jnp.int32)).astype(F32)[:, None] * inv[None, :]
    cos, sin = jnp.cos(ang), jnp.sin(ang)
    reps = LANES // HEAD_DIM
    return (jnp.tile(jnp.concatenate([cos, cos], axis=-1), (1, reps)),
            jnp.tile(jnp.concatenate([-sin, sin], axis=-1), (1, reps)))


def _prepare_params(norm1, w_in, conv_w, conv_b, w_rg, b_rg, w_ig, b_ig, lru_lambda,
                    ssm_a_re, ssm_a_im, ssm_b_re, ssm_b_im, ssm_c_re, ssm_c_im, ssm_d, ssm_log_dt,
                    w_glu, b_glu, attn_sinks, w_out, norm2, w_up, w_down):
    depth = w_in.shape[0]
    dt = jnp.exp(ssm_log_dt)[..., None]
    lr, li = ssm_a_re, ssm_a_im
    mag = jnp.exp(lr * dt)
    abar_r, abar_i = mag * jnp.cos(li * dt), mag * jnp.sin(li * dt)
    den = lr * lr + li * li
    nr = abar_r - 1.0
    fr = (nr * lr + abar_i * li) / den
    fi = (abar_i * lr - nr * li) / den
    bb_r = fr[..., None] * ssm_b_re - fi[..., None] * ssm_b_im
    bb_i = fr[..., None] * ssm_b_im + fi[..., None] * ssm_b_re
    bb = jnp.concatenate([_block_diag(jnp.swapaxes(bb_r, -1, -2)), _block_diag(jnp.swapaxes(bb_i, -1, -2))], axis=-1)
    cc = jnp.concatenate([_block_diag(jnp.swapaxes(ssm_c_re, -1, -2)),
                          -_block_diag(jnp.swapaxes(ssm_c_im, -1, -2))], axis=-2)
    row = lambda a: a[:, None, :]
    return dict(
        g1=row(norm1), w_in=w_in.astype(BF16),
        convw=conv_w, convb=row(conv_b),
        wgate=jnp.concatenate([_block_diag(w_rg), _block_diag(w_ig)], axis=-1).astype(BF16),
        bgate=row(jnp.concatenate([b_rg, b_ig], axis=-1)),
        lamc=row(-LRU_C * jax.nn.softplus(-lru_lambda)),
        abar=jnp.stack([abar_r.reshape(depth, D_STATE), abar_i.reshape(depth, D_STATE)], axis=1),
        bb=bb.astype(BF16), cc=cc.astype(BF16), d=row(ssm_d),
        wglu=w_glu.astype(BF16), bglu=row(b_glu),
        sinks=attn_sinks, w_out=w_out.astype(BF16), g2=row(norm2),
        w_up=w_up.astype(BF16), w_down=w_down.astype(BF16),
    )


def _layer(x, p, layer, tables, conv_buf, h0, s_re, s_im, cache_k, cache_v, gf, *, pos_base, final, CS):
    NB, L, _ = x.shape
    pab, oc, k_new, v_new = _inproj_attn(x, p, layer, tables[0], tables[1], cache_k, cache_v,
                                         pos_base=pos_base, T=min(T_ATTN, L), CS=CS)
    cbuf0 = jnp.swapaxes(conv_buf, 0, 1).reshape((CONV_W - 1) * NB, D_LRU)
    oab, cbuf, h_last, sr, si = _lru_s5(
        pab, cbuf0, h0, s_re.reshape(NB, D_STATE), s_im.reshape(NB, D_STATE), p, layer, T=min(T_SCAN, L))
    if NB * L <= T_MLP:
        fold = lambda a: a.reshape(1, NB * L, a.shape[-1])
        x_new = _outproj_mlp(fold(x), fold(oab), fold(oc), p, layer, gf,
                             final=final, T=NB * L).reshape(NB, L, D_MODEL)
    else:
        x_new = _outproj_mlp(x, oab, oc, p, layer, gf, final=final, T=min(T_MLP, L))
    n_out = k_new.shape[1]
    state = (jnp.swapaxes(cbuf.reshape(CONV_W - 1, NB, D_LRU), 0, 1), h_last,
             sr.reshape(NB, SSM_GROUPS, SSM_STATE), si.reshape(NB, SSM_GROUPS, SSM_STATE),
             k_new.reshape(NB, n_out, N_KV_HEADS, HEAD_DIM), v_new.reshape(NB, n_out, N_KV_HEADS, HEAD_DIM))
    return x_new, state


def kernel(x_prompt, x_sample, cache_conv_a, state_lru, state_ssm_re, state_ssm_im, cache_k, cache_v, norm1, w_in, conv_w, conv_b, w_rg, b_rg, w_ig, b_ig, lru_lambda, ssm_a_re, ssm_a_im, ssm_b_re, ssm_b_im, ssm_c_re, ssm_c_im, ssm_d, ssm_log_dt, w_glu, b_glu, attn_sinks, w_out, norm2, w_up, w_down, norm_f):
    Bp, Lp = x_prompt.shape[0], x_prompt.shape[1]
    Bs, Ls = x_sample.shape[0], x_sample.shape[1]
    depth = w_in.shape[0]
    tab_p = _rope_tables(0, Lp)
    tab_s = _rope_tables(PAST_LEN, Ls)
    gf = norm_f[None, :]
    zconv = jnp.zeros((Bp, CONV_W - 1, D_LRU), F32)
    zh = jnp.zeros((Bp, D_LRU), F32)
    zs = jnp.zeros((Bp, SSM_GROUPS, SSM_STATE), F32)
    zkv = jnp.zeros((Bp, WINDOW, D_KV), F32)

    p = _prepare_params(norm1, w_in, conv_w, conv_b, w_rg, b_rg, w_ig, b_ig, lru_lambda,
                        ssm_a_re, ssm_a_im, ssm_b_re, ssm_b_im, ssm_c_re, ssm_c_im, ssm_d, ssm_log_dt,
                        w_glu, b_glu, attn_sinks, w_out, norm2, w_up, w_down)
    xp, xs = x_prompt, x_sample
    st_p, st_s = [], []
    for l in range(depth):
        final = l == depth - 1
        xp, sp = _layer(xp, p, l, tab_p, zconv, zh, zs, zs, zkv, zkv, gf, pos_base=0, final=final, CS=CHUNK)
        xs, ss = _layer(xs, p, l, tab_s, cache_conv_a[l], state_lru[l], state_ssm_re[l], state_ssm_im[l],
                        cache_k[l].reshape(Bs, WINDOW, D_KV), cache_v[l].reshape(Bs, WINDOW, D_KV), gf,
                        pos_base=PAST_LEN, final=final, CS=Ls)
        st_p.append(sp)
        st_s.append(ss)

    outs = [xp, xs]
    for sts in (st_p, st_s):
        for i in range(6):
            outs.append(jnp.stack([s[i] for s in sts], 0))
    return tuple(outs)
```

```python
import functools

import jax
import jax.numpy as jnp
from jax import lax
from jax.experimental import pallas as pl
from jax.experimental.pallas import tpu as pltpu

F32 = jnp.float32
BF16 = jnp.bfloat16

D_MODEL = 1024
PAST_LEN = 4096
CHUNK = 64
D_LRU = 256
CONV_W = 4
LRU_C = 8.0
D_SSM = 256
SSM_GROUPS = 16
SSM_STATE = 64
D_STATE = SSM_GROUPS * SSM_STATE
HEAD_DIM = 64
N_Q_HEADS = 8
N_KV_HEADS = 2
D_ATT = N_Q_HEADS * HEAD_DIM
D_KV = N_KV_HEADS * HEAD_DIM
WINDOW = 128
ROPE_THETA = 10000.0
D_AB = 2 * D_LRU + D_SSM
D_IN = D_AB + D_ATT + 2 * D_KV
D_FF = 4 * D_MODEL
RMS_EPS = 1e-6
NEG_INF = -1e30
LOG2E = 1.4426950408889634

LANES = 128
KEY_WIN = 2 * LANES
FF_CHUNK = 1024
SCAN_LANES = 256
SCAN_SUB = 128
VMEM_LIMIT = 56 * 1024 * 1024
VMEM_LIMIT_MLP = 60 * 1024 * 1024

T_ATTN = 256
ROWS_ATTN = 2048
PROJ_ROWS = 256
T_SCAN = 128
T_MLP = 1024


def _rms(x, g):
    return (x * lax.rsqrt(jnp.mean(x * x, axis=-1, keepdims=True) + RMS_EPS)) * g


def _sigmoid(x):
    return 0.5 * jnp.tanh(0.5 * x) + 0.5


def _const_spec(shape):
    nd = len(shape)
    return pl.BlockSpec(shape, lambda *_: (0,) * nd, pipeline_mode=pl.Buffered(1))


def _layer_spec(stacked, layer):
    nd = stacked.ndim - 1
    return pl.BlockSpec((None,) + stacked.shape[1:], lambda *_: (layer,) + (0,) * nd, pipeline_mode=pl.Buffered(1))


def _inproj_attn_kernel(pos_base, T, CS, G, layer, x_ref, g1_ref, win_ref, cos_ref, sin_ref, ck_ref, cv_ref, sink_ref,
                        pab_ref, oc_ref, kout_ref, vout_ref, kext_all, vext_all):
    tc = pl.program_id(1)

    @pl.when(tc == 0)
    def _():
        kext_all[...] = jnp.zeros(kext_all.shape, F32)
        vext_all[...] = jnp.zeros(vext_all.shape, F32)
        kext_all[:, 0:WINDOW, :] = ck_ref[...]
        vext_all[:, 0:WINDOW, :] = cv_ref[...]

    cos = cos_ref[...]
    sin = sin_ref[...]
    lane = lax.broadcasted_iota(jnp.int32, (1, LANES), 1)
    first_half = (lane % HEAD_DIM) < (HEAD_DIM // 2)
    low_head = lane < HEAD_DIM
    scale = HEAD_DIM ** -0.5 * LOG2E
    n_out = kout_ref.shape[1]

    def rope(a):
        partner = jnp.where(first_half, pltpu.roll(a, LANES - HEAD_DIM // 2, 1), pltpu.roll(a, HEAD_DIM // 2, 1))
        return a * cos + partner * sin

    def variants(full):
        swapped = pltpu.roll(full, HEAD_DIM, 1)
        zero = jnp.zeros_like(full)
        return ((jnp.where(low_head, full, zero).astype(BF16), jnp.where(low_head, zero, swapped).astype(BF16)),
                (jnp.where(low_head, swapped, zero).astype(BF16), jnp.where(low_head, zero, full).astype(BF16)))

    PG = min(G, max(1, PROJ_ROWS // T))
    queries = []
    for g in range(G):
        kext, vext = kext_all.at[g], vext_all.at[g]
        if g % PG == 0:
            hn = _rms(x_ref[g:g + PG].reshape(PG * T, D_MODEL), g1_ref[...])
            proj_group = jnp.dot(hn.astype(BF16), win_ref[...], preferred_element_type=F32)
        proj = proj_group[(g % PG) * T:(g % PG + 1) * T]
        pab_ref[g] = proj[:, :D_AB]
        queries.append([(rope(proj[:, D_AB + j * LANES:D_AB + (j + 1) * LANES]) * scale).astype(BF16)
                        for j in range(D_ATT // LANES)])
        kext[WINDOW:WINDOW + T, :] = rope(proj[:, D_AB + D_ATT:D_AB + D_ATT + D_KV])
        vext[WINDOW:WINDOW + T, :] = proj[:, D_AB + D_ATT + D_KV:]
        kout_ref[g] = kext[WINDOW + T - n_out:WINDOW + T, :]
        vout_ref[g] = vext[WINDOW + T - n_out:WINDOW + T, :]

    kidx = lax.broadcasted_iota(jnp.int32, (1, KEY_WIN), 1)
    row = lax.broadcasted_iota(jnp.int32, (2 * CS, 1), 0)
    for g in range(G):
        kext, vext = kext_all.at[g], vext_all.at[g]
        qcols = queries[g]
        kvar = variants(kext[...])
        vvar = variants(vext[...])
        for c in range(T // CS):
            lo = c * CS
            pos0 = pos_base + tc * T + lo
            valid = (kidx >= WINDOW - pos0) & (kidx < WINDOW + CS)
            for kv in range(N_KV_HEADS):
                q2 = jnp.concatenate([qcols[2 * kv][lo:lo + CS], qcols[2 * kv + 1][lo:lo + CS]], axis=0)
                out = None
                for half in range(2):
                    kk = kvar[kv][half][lo:lo + KEY_WIN]
                    vv = vvar[kv][half][lo:lo + KEY_WIN]
                    s = lax.dot_general(q2, kk, (((1,), (1,)), ((), ())), preferred_element_type=F32)
                    s = jnp.where(valid, s, NEG_INF)
                    h0 = 4 * kv + half
                    sink = jnp.where(row < CS, sink_ref[layer, h0], sink_ref[layer, h0 + 2]) * LOG2E
                    m = jnp.maximum(jnp.max(s, axis=-1, keepdims=True), sink)
                    e = jnp.exp2(s - m)
                    denom = jnp.sum(e, axis=-1, keepdims=True) + jnp.exp2(sink - m)
                    o = jnp.dot(e.astype(BF16), vv, preferred_element_type=F32) / denom
                    out = o if out is None else out + o
                oc_ref[g, lo:lo + CS, 2 * kv * LANES:(2 * kv + 1) * LANES] = out[:CS].astype(BF16)
                oc_ref[g, lo:lo + CS, (2 * kv + 1) * LANES:(2 * kv + 2) * LANES] = out[CS:].astype(BF16)
        kext[0:WINDOW, :] = kext[T:T + WINDOW, :]
        vext[0:WINDOW, :] = vext[T:T + WINDOW, :]


def _inproj_attn(x, p, layer, cos_t, sin_t, cache_k, cache_v, *, pos_base, T, CS):
    NB, L, _ = x.shape
    n_out = min(L, WINDOW)
    ext_rows = T - CS + KEY_WIN
    G = max(g for g in range(1, NB + 1) if NB % g == 0 and g * T <= max(ROWS_ATTN, T))
    kern = functools.partial(_inproj_attn_kernel, pos_base, T, CS, G, layer)
    stream_block = lambda rows, width: pl.BlockSpec((G, rows, width), lambda b, t: (b, t, 0))
    stream_const = lambda rows, width: pl.BlockSpec((G, rows, width), lambda b, t: (b, 0, 0))
    return pl.pallas_call(
        kern,
        grid=(NB // G, L // T),
        in_specs=[
            stream_block(T, D_MODEL),
            _layer_spec(p['g1'], layer),
            _layer_spec(p['w_in'], layer),
            pl.BlockSpec((T, LANES), lambda b, t: (t, 0)),
            pl.BlockSpec((T, LANES), lambda b, t: (t, 0)),
            stream_const(WINDOW, D_KV),
            stream_const(WINDOW, D_KV),
            pl.BlockSpec(memory_space=pltpu.SMEM),
        ],
        out_specs=[
            stream_block(T, D_AB),
            stream_block(T, D_ATT),
            stream_const(n_out, D_KV),
            stream_const(n_out, D_KV),
        ],
        out_shape=[
            jax.ShapeDtypeStruct((NB, L, D_AB), F32),
            jax.ShapeDtypeStruct((NB, L, D_ATT), BF16),
            jax.ShapeDtypeStruct((NB, n_out, D_KV), F32),
            jax.ShapeDtypeStruct((NB, n_out, D_KV), F32),
        ],
        scratch_shapes=[pltpu.VMEM((G, ext_rows, D_KV), F32), pltpu.VMEM((G, ext_rows, D_KV), F32)],
        compiler_params=pltpu.CompilerParams(dimension_semantics=("arbitrary", "arbitrary"),
                                             vmem_limit_bytes=VMEM_LIMIT),
        name="inproj_attn",
    )(x, p['g1'], p['w_in'], cos_t, sin_t, cache_k, cache_v, p['sinks'])


def _lru_s5_kernel(NB, T, TS, pab_ref, cbuf0_ref, h0_ref, sr0_ref, si0_ref,
                   convw_ref, convb_ref, wgate_ref, bgate_ref, lamc_ref,
                   abar_ref, bb_ref, cc_ref, d_ref, wglu_ref, bglu_ref,
                   oab_ref, cbuf_out, h_out, sr_out, si_out,
                   uext, gu_s, a_s, b_s, bu_s, h_st, s_st):
    tc = pl.program_id(0)
    R = T * NB
    RS = TS * NB
    TAIL = (CONV_W - 1) * NB

    @pl.when(tc == 0)
    def _():
        uext[0:TAIL, :] = cbuf0_ref[...]
        h_st[...] = h0_ref[...]
        s_st[:, 0:D_STATE] = sr0_ref[...]
        s_st[:, D_STATE:] = si0_ref[...]

    def to_time_major(t0, c0):
        return jnp.swapaxes(pab_ref[:, t0:t0 + TS, c0:c0 + D_LRU], 0, 1).reshape(RS, D_LRU)

    for sb in range(T // TS):
        r0 = sb * RS
        uext[TAIL + r0:TAIL + r0 + RS, :] = to_time_major(sb * TS, 0)
        gu_s[r0:r0 + RS, 0:D_LRU] = to_time_major(sb * TS, D_LRU)
        gu_s[r0:r0 + RS, D_LRU:] = to_time_major(sb * TS, 2 * D_LRU)

    for sb in range(T // TS):
        r0 = sb * RS
        xc = convb_ref[...]
        for j in range(CONV_W):
            xc = xc + uext[j * NB + r0:j * NB + r0 + RS, :] * convw_ref[j:j + 1, :]
        g = jnp.dot(xc.astype(BF16), wgate_ref[...], preferred_element_type=F32) + bgate_ref[...]
        r_gate = _sigmoid(g[:, :D_LRU])
        i_gate = _sigmoid(g[:, D_LRU:])
        log_a = r_gate * lamc_ref[...]
        a_s[r0:r0 + RS, :] = jnp.exp(log_a)
        b_s[r0:r0 + RS, :] = jnp.sqrt(1.0 - jnp.exp(2.0 * log_a)) * (i_gate * xc)
        bu_s[r0:r0 + RS, :] = jnp.dot(gu_s[r0:r0 + RS, D_LRU:].astype(BF16), bb_ref[...],
                                      preferred_element_type=F32)

    tail = uext[R:R + TAIL, :]
    uext[0:TAIL, :] = tail
    cbuf_out[...] = tail

    def lru_step(t, h):
        r0 = pl.multiple_of(t * NB, NB)
        h = a_s[pl.ds(r0, NB), :] * h + b_s[pl.ds(r0, NB), :]
        b_s[pl.ds(r0, NB), :] = h
        return h

    h_last = lax.fori_loop(0, T, lru_step, h_st[...], unroll=True)
    h_st[...] = h_last
    h_out[...] = h_last

    for ch in range(D_STATE // SCAN_LANES):
        c0 = ch * SCAN_LANES
        ar = jnp.broadcast_to(abar_ref[0:1, c0:c0 + SCAN_LANES], (NB, SCAN_LANES))
        ai = jnp.broadcast_to(abar_ref[1:2, c0:c0 + SCAN_LANES], (NB, SCAN_LANES))

        def s5_step(t, carry, c0=c0, ar=ar, ai=ai):
            sr, si = carry
            r0 = pl.multiple_of(t * NB, NB)
            nr = ar * sr - ai * si + bu_s[pl.ds(r0, NB), c0:c0 + SCAN_LANES]
            ni = ar * si + ai * sr + bu_s[pl.ds(r0, NB), D_STATE + c0:D_STATE + c0 + SCAN_LANES]
            bu_s[pl.ds(r0, NB), c0:c0 + SCAN_LANES] = nr
            bu_s[pl.ds(r0, NB), D_STATE + c0:D_STATE + c0 + SCAN_LANES] = ni
            return nr, ni

        sr, si = lax.fori_loop(0, T, s5_step,
                               (s_st[:, c0:c0 + SCAN_LANES], s_st[:, D_STATE + c0:D_STATE + c0 + SCAN_LANES]),
                               unroll=True)
        s_st[:, c0:c0 + SCAN_LANES] = sr
        s_st[:, D_STATE + c0:D_STATE + c0 + SCAN_LANES] = si

    sr_out[...] = s_st[:, 0:D_STATE]
    si_out[...] = s_st[:, D_STATE:]

    def to_batch_major(v):
        return jnp.swapaxes(v.reshape(TS, NB, D_LRU), 0, 1).astype(BF16)

    for sb in range(T // TS):
        r0 = sb * RS
        t0 = sb * TS
        out_a = b_s[r0:r0 + RS, :] * jax.nn.gelu(gu_s[r0:r0 + RS, 0:D_LRU])
        oab_ref[:, t0:t0 + TS, 0:D_LRU] = to_batch_major(out_a)
        y = jnp.dot(bu_s[r0:r0 + RS, :].astype(BF16), cc_ref[...], preferred_element_type=F32)
        y = y + d_ref[...] * gu_s[r0:r0 + RS, D_LRU:]
        z = jax.nn.gelu(y)
        gl = jnp.dot(z.astype(BF16), wglu_ref[...], preferred_element_type=F32) + bglu_ref[...]
        oab_ref[:, t0:t0 + TS, D_LRU:] = to_batch_major(z * _sigmoid(gl))


def _lru_s5(pab, cbuf0, h0, sr0, si0, p, layer, *, T):
    NB, L, _ = pab.shape
    R = T * NB
    TAIL = (CONV_W - 1) * NB
    TS = min(T, SCAN_SUB)
    kern = functools.partial(_lru_s5_kernel, NB, T, TS)
    states = [cbuf0, h0, sr0, si0]
    params = [p[k] for k in ('convw', 'convb', 'wgate', 'bgate', 'lamc', 'abar', 'bb', 'cc', 'd', 'wglu', 'bglu')]
    return pl.pallas_call(
        kern,
        grid=(L // T,),
        in_specs=([pl.BlockSpec((NB, T, D_AB), lambda t: (0, t, 0))] + [_const_spec(c.shape) for c in states]
                  + [_layer_spec(c, layer) for c in params]),
        out_specs=[
            pl.BlockSpec((NB, T, 2 * D_LRU), lambda t: (0, t, 0)),
            pl.BlockSpec((TAIL, D_LRU), lambda t: (0, 0)),
            pl.BlockSpec((NB, D_LRU), lambda t: (0, 0)),
            pl.BlockSpec((NB, D_STATE), lambda t: (0, 0)),
            pl.BlockSpec((NB, D_STATE), lambda t: (0, 0)),
        ],
        out_shape=[
            jax.ShapeDtypeStruct((NB, L, 2 * D_LRU), BF16),
            jax.ShapeDtypeStruct((TAIL, D_LRU), F32),
            jax.ShapeDtypeStruct((NB, D_LRU), F32),
            jax.ShapeDtypeStruct((NB, D_STATE), F32),
            jax.ShapeDtypeStruct((NB, D_STATE), F32),
        ],
        scratch_shapes=[
            pltpu.VMEM((R + TAIL, D_LRU), F32),
            pltpu.VMEM((R, 2 * D_LRU), F32),
            pltpu.VMEM((R, D_LRU), F32),
            pltpu.VMEM((R, D_LRU), F32),
            pltpu.VMEM((R, 2 * D_STATE), F32),
            pltpu.VMEM((NB, D_LRU), F32),
            pltpu.VMEM((NB, 2 * D_STATE), F32),
        ],
        compiler_params=pltpu.CompilerParams(dimension_semantics=("arbitrary",), vmem_limit_bytes=VMEM_LIMIT),
        name="lru_s5",
    )(pab, *states, *params)


def _outproj_mlp_kernel(final, x_ref, oab_ref, oc_ref, wout_ref, g2_ref, wup_ref, wdn_ref, gf_ref, y_ref):
    x = x_ref[0]
    mix = jnp.dot(oab_ref[0], wout_ref[0:2 * D_LRU, :], preferred_element_type=F32)
    mix = mix + jnp.dot(oc_ref[0], wout_ref[2 * D_LRU:, :], preferred_element_type=F32)
    x = x + mix
    hm = _rms(x, g2_ref[...]).astype(BF16)
    down = None
    for c in range(D_FF // FF_CHUNK):
        up = jnp.dot(hm, wup_ref[:, c * FF_CHUNK:(c + 1) * FF_CHUNK], preferred_element_type=F32)
        act = jnp.square(jnp.maximum(up, 0.0)).astype(BF16)
        part = jnp.dot(act, wdn_ref[c * FF_CHUNK:(c + 1) * FF_CHUNK, :], preferred_element_type=F32)
        down = part if down is None else down + part
    x = x + down
    if final:
        x = _rms(x, gf_ref[...])
    y_ref[0] = x


def _outproj_mlp(x, oab, oc, p, layer, gf, *, final, T):
    NB, L, _ = x.shape
    kern = functools.partial(_outproj_mlp_kernel, final)
    row_block = lambda width: pl.BlockSpec((1, T, width), lambda b, t: (b, t, 0))
    return pl.pallas_call(
        kern,
        grid=(NB, L // T),
        in_specs=[
            row_block(D_MODEL), row_block(2 * D_LRU), row_block(D_ATT),
            _layer_spec(p['w_out'], layer), _layer_spec(p['g2'], layer),
            _layer_spec(p['w_up'], layer), _layer_spec(p['w_down'], layer),
            _const_spec((1, D_MODEL)),
        ],
        out_specs=row_block(D_MODEL),
        out_shape=jax.ShapeDtypeStruct((NB, L, D_MODEL), F32),
        compiler_params=pltpu.CompilerParams(dimension_semantics=("arbitrary", "arbitrary"),
                                             vmem_limit_bytes=VMEM_LIMIT_MLP),
        name="outproj_mlp",
    )(x, oab, oc, p['w_out'], p['g2'], p['w_up'], p['w_down'], gf)


def _block_diag(blocks):
    n, r, c = blocks.shape[-3:]
    on_diag = jnp.eye(n, dtype=bool)[:, None, :, None]
    full = jnp.where(on_diag, blocks[..., :, :, None, :], jnp.zeros((), blocks.dtype))
    return full.reshape(blocks.shape[:-3] + (n * r, n * c))


def _rope_tables(pos_base, L):
    half = HEAD_DIM // 2
    inv = ROPE_THETA ** (-jnp.arange(half, dtype=F32) / half)
    ang = (pos_base + jnp.arange(L, dtype=jnp.int32)).astype(F32)[:, None] * inv[None, :]
    cos, sin = jnp.cos(ang), jnp.sin(ang)
    reps = LANES // HEAD_DIM
    return (jnp.tile(jnp.concatenate([cos, cos], axis=-1), (1, reps)),
            jnp.tile(jnp.concatenate([-sin, sin], axis=-1), (1, reps)))


def _prepare_params(norm1, w_in, conv_w, conv_b, w_rg, b_rg, w_ig, b_ig, lru_lambda,
                    ssm_a_re, ssm_a_im, ssm_b_re, ssm_b_im, ssm_c_re, ssm_c_im, ssm_d, ssm_log_dt,
                    w_glu, b_glu, attn_sinks, w_out, norm2, w_up, w_down):
    depth = w_in.shape[0]
    dt = jnp.exp(ssm_log_dt)[..., None]
    lr, li = ssm_a_re, ssm_a_im
    mag = jnp.exp(lr * dt)
    abar_r, abar_i = mag * jnp.cos(li * dt), mag * jnp.sin(li * dt)
    den = lr * lr + li * li
    nr = abar_r - 1.0
    fr = (nr * lr + abar_i * li) / den
    fi = (abar_i * lr - nr * li) / den
    bb_r = fr[..., None] * ssm_b_re - fi[..., None] * ssm_b_im
    bb_i = fr[..., None] * ssm_b_im + fi[..., None] * ssm_b_re
    bb = jnp.concatenate([_block_diag(jnp.swapaxes(bb_r, -1, -2)), _block_diag(jnp.swapaxes(bb_i, -1, -2))], axis=-1)
    cc = jnp.concatenate([_block_diag(jnp.swapaxes(ssm_c_re, -1, -2)),
                          -_block_diag(jnp.swapaxes(ssm_c_im, -1, -2))], axis=-2)
    row = lambda a: a[:, None, :]
    return dict(
        g1=row(norm1), w_in=w_in.astype(BF16),
        convw=conv_w, convb=row(conv_b),
        wgate=jnp.concatenate([_block_diag(w_rg), _block_diag(w_ig)], axis=-1).astype(BF16),
        bgate=row(jnp.concatenate([b_rg, b_ig], axis=-1)),
        lamc=row(-LRU_C * jax.nn.softplus(-lru_lambda)),
        abar=jnp.stack([abar_r.reshape(depth, D_STATE), abar_i.reshape(depth, D_STATE)], axis=1),
        bb=bb.astype(BF16), cc=cc.astype(BF16), d=row(ssm_d),
        wglu=w_glu.astype(BF16), bglu=row(b_glu),
        sinks=attn_sinks, w_out=w_out.astype(BF16), g2=row(norm2),
        w_up=w_up.astype(BF16), w_down=w_down.astype(BF16),
    )


def _layer(x, p, layer, tables, conv_buf, h0, s_re, s_im, cache_k, cache_v, gf, *, pos_base, final, CS):
    NB, L, _ = x.shape
    pab, oc, k_new, v_new = _inproj_attn(x, p, layer, tables[0], tables[1], cache_k, cache_v,
                                         pos_base=pos_base, T=min(T_ATTN, L), CS=CS)
    cbuf0 = jnp.swapaxes(conv_buf, 0, 1).reshape((CONV_W - 1) * NB, D_LRU)
    oab, cbuf, h_last, sr, si = _lru_s5(
        pab, cbuf0, h0, s_re.reshape(NB, D_STATE), s_im.reshape(NB, D_STATE), p, layer, T=min(T_SCAN, L))
    if NB * L <= T_MLP:
        fold = lambda a: a.reshape(1, NB * L, a.shape[-1])
        x_new = _outproj_mlp(fold(x), fold(oab), fold(oc), p, layer, gf,
                             final=final, T=NB * L).reshape(NB, L, D_MODEL)
    else:
        x_new = _outproj_mlp(x, oab, oc, p, layer, gf, final=final, T=min(T_MLP, L))
    n_out = k_new.shape[1]
    state = (jnp.swapaxes(cbuf.reshape(CONV_W - 1, NB, D_LRU), 0, 1), h_last,
             sr.reshape(NB, SSM_GROUPS, SSM_STATE), si.reshape(NB, SSM_GROUPS, SSM_STATE),
             k_new.reshape(NB, n_out, N_KV_HEADS, HEAD_DIM), v_new.reshape(NB, n_out, N_KV_HEADS, HEAD_DIM))
    return x_new, state


def kernel(x_prompt, x_sample, cache_conv_a, state_lru, state_ssm_re, state_ssm_im, cache_k, cache_v, norm1, w_in, conv_w, conv_b, w_rg, b_rg, w_ig, b_ig, lru_lambda, ssm_a_re, ssm_a_im, ssm_b_re, ssm_b_im, ssm_c_re, ssm_c_im, ssm_d, ssm_log_dt, w_glu, b_glu, attn_sinks, w_out, norm2, w_up, w_down, norm_f):
    Bp, Lp = x_prompt.shape[0], x_prompt.shape[1]
    Bs, Ls = x_sample.shape[0], x_sample.shape[1]
    depth = w_in.shape[0]
    tab_p = _rope_tables(0, Lp)
    tab_s = _rope_tables(PAST_LEN, Ls)
    gf = norm_f[None, :]
    zconv = jnp.zeros((Bp, CONV_W - 1, D_LRU), F32)
    zh = jnp.zeros((Bp, D_LRU), F32)
    zs = jnp.zeros((Bp, SSM_GROUPS, SSM_STATE), F32)
    zkv = jnp.zeros((Bp, WINDOW, D_KV), F32)

    p = _prepare_params(norm1, w_in, conv_w, conv_b, w_rg, b_rg, w_ig, b_ig, lru_lambda,
                        ssm_a_re, ssm_a_im, ssm_b_re, ssm_b_im, ssm_c_re, ssm_c_im, ssm_d, ssm_log_dt,
                        w_glu, b_glu, attn_sinks, w_out, norm2, w_up, w_down)
    xp, xs = x_prompt, x_sample
    st_p, st_s = [], []
    for l in range(depth):
        final = l == depth - 1
        xp, sp = _layer(xp, p, l, tab_p, zconv, zh, zs, zs, zkv, zkv, gf, pos_base=0, final=final, CS=CHUNK)
        xs, ss = _layer(xs, p, l, tab_s, cache_conv_a[l], state_lru[l], state_ssm_re[l], state_ssm_im[l],
                        cache_k[l].reshape(Bs, WINDOW, D_KV), cache_v[l].reshape(Bs, WINDOW, D_KV), gf,
                        pos_base=PAST_LEN, final=final, CS=Ls)
        st_p.append(sp)
        st_s.append(ss)

    outs = [xp, xs]
    for sts in (st_p, st_s):
        for i in range(6):
            outs.append(jnp.stack([s[i] for s in sts], 0))
    return tuple(outs)
```

```python
import functools

import jax
import jax.numpy as jnp
from jax import lax
from jax.experimental import pallas as pl
from jax.experimental.pallas import tpu as pltpu

F32 = jnp.float32
BF16 = jnp.bfloat16

D_MODEL = 1024
PAST_LEN = 4096
CHUNK = 64
D_LRU = 256
CONV_W = 4
LRU_C = 8.0
D_SSM = 256
SSM_GROUPS = 16
SSM_STATE = 64
D_STATE = SSM_GROUPS * SSM_STATE
HEAD_DIM = 64
N_Q_HEADS = 8
N_KV_HEADS = 2
D_ATT = N_Q_HEADS * HEAD_DIM
D_KV = N_KV_HEADS * HEAD_DIM
WINDOW = 128
ROPE_THETA = 10000.0
D_AB = 2 * D_LRU + D_SSM
D_IN = D_AB + D_ATT + 2 * D_KV
D_FF = 4 * D_MODEL
RMS_EPS = 1e-6
NEG_INF = -1e30
LOG2E = 1.4426950408889634

LANES = 128
KEY_WIN = 2 * LANES
FF_CHUNK = 1024
SCAN_LANES = 256
SCAN_SUB = 128
VMEM_LIMIT = 56 * 1024 * 1024
VMEM_LIMIT_MLP = 60 * 1024 * 1024

T_ATTN = 256
ROWS_ATTN = 2048
PROJ_ROWS = 256
T_SCAN = 128
T_MLP = 1024


def _rms(x, g):
    return (x * lax.rsqrt(jnp.mean(x * x, axis=-1, keepdims=True) + RMS_EPS)) * g


def _sigmoid(x):
    return 0.5 * jnp.tanh(0.5 * x) + 0.5


def _const_spec(shape):
    nd = len(shape)
    return pl.BlockSpec(shape, lambda *_: (0,) * nd, pipeline_mode=pl.Buffered(1))


def _layer_spec(stacked, layer):
    nd = stacked.ndim - 1
    return pl.BlockSpec((None,) + stacked.shape[1:], lambda *_: (layer,) + (0,) * nd, pipeline_mode=pl.Buffered(1))


def _inproj_attn_kernel(pos_base, T, CS, G, layer, x_ref, g1_ref, win_ref, cos_ref, sin_ref, ck_ref, cv_ref, sink_ref,
                        pab_ref, oc_ref, kout_ref, vout_ref, kext_all, vext_all):
    tc = pl.program_id(1)

    @pl.when(tc == 0)
    def _():
        kext_all[...] = jnp.zeros(kext_all.shape, F32)
        vext_all[...] = jnp.zeros(vext_all.shape, F32)
        kext_all[:, 0:WINDOW, :] = ck_ref[...]
        vext_all[:, 0:WINDOW, :] = cv_ref[...]

    cos = cos_ref[...]
    sin = sin_ref[...]
    lane = lax.broadcasted_iota(jnp.int32, (1, LANES), 1)
    first_half = (lane % HEAD_DIM) < (HEAD_DIM // 2)
    low_head = lane < HEAD_DIM
    scale = HEAD_DIM ** -0.5 * LOG2E
    n_out = kout_ref.shape[1]

    def rope(a):
        partner = jnp.where(first_half, pltpu.roll(a, LANES - HEAD_DIM // 2, 1), pltpu.roll(a, HEAD_DIM // 2, 1))
        return a * cos + partner * sin

    def variants(full):
        swapped = pltpu.roll(full, HEAD_DIM, 1)
        zero = jnp.zeros_like(full)
        return ((jnp.where(low_head, full, zero).astype(BF16), jnp.where(low_head, zero, swapped).astype(BF16)),
                (jnp.where(low_head, swapped, zero).astype(BF16), jnp.where(low_head, zero, full).astype(BF16)))

    PG = min(G, max(1, PROJ_ROWS // T))
    queries = []
    for g in range(G):
        kext, vext = kext_all.at[g], vext_all.at[g]
        if g % PG == 0:
            hn = _rms(x_ref[g:g + PG].reshape(PG * T, D_MODEL), g1_ref[...])
            proj_group = jnp.dot(hn.astype(BF16), win_ref[...], preferred_element_type=F32)
        proj = proj_group[(g % PG) * T:(g % PG + 1) * T]
        pab_ref[g] = proj[:, :D_AB]
        queries.append([(rope(proj[:, D_AB + j * LANES:D_AB + (j + 1) * LANES]) * scale).astype(BF16)
                        for j in range(D_ATT // LANES)])
        kext[WINDOW:WINDOW + T, :] = rope(proj[:, D_AB + D_ATT:D_AB + D_ATT + D_KV])
        vext[WINDOW:WINDOW + T, :] = proj[:, D_AB + D_ATT + D_KV:]
        kout_ref[g] = kext[WINDOW + T - n_out:WINDOW + T, :]
        vout_ref[g] = vext[WINDOW + T - n_out:WINDOW + T, :]

    kidx = lax.broadcasted_iota(jnp.int32, (1, KEY_WIN), 1)
    row = lax.broadcasted_iota(jnp.int32, (2 * CS, 1), 0)
    for g in range(G):
        kext, vext = kext_all.at[g], vext_all.at[g]
        qcols = queries[g]
        kvar = variants(kext[...])
        vvar = variants(vext[...])
        for c in range(T // CS):
            lo = c * CS
            pos0 = pos_base + tc * T + lo
            valid = (kidx >= WINDOW - pos0) & (kidx < WINDOW + CS)
            for kv in range(N_KV_HEADS):
                q2 = jnp.concatenate([qcols[2 * kv][lo:lo + CS], qcols[2 * kv + 1][lo:lo + CS]], axis=0)
                out = None
                for half in range(2):
                    kk = kvar[kv][half][lo:lo + KEY_WIN]
                    vv = vvar[kv][half][lo:lo + KEY_WIN]
                    s = lax.dot_general(q2, kk, (((1,), (1,)), ((), ())), preferred_element_type=F32)
                    s = jnp.where(valid, s, NEG_INF)
                    h0 = 4 * kv + half
                    sink = jnp.where(row < CS, sink_ref[layer, h0], sink_ref[layer, h0 + 2]) * LOG2E
                    m = jnp.maximum(jnp.max(s, axis=-1, keepdims=True), sink)
                    e = jnp.exp2(s - m)
                    denom = jnp.sum(e, axis=-1, keepdims=True) + jnp.exp2(sink - m)
                    o = jnp.dot(e.astype(BF16), vv, preferred_element_type=F32) / denom
                    out = o if out is None else out + o
                oc_ref[g, lo:lo + CS, 2 * kv * LANES:(2 * kv + 1) * LANES] = out[:CS].astype(BF16)
                oc_ref[g, lo:lo + CS, (2 * kv + 1) * LANES:(2 * kv + 2) * LANES] = out[CS:].astype(BF16)
        kext[0:WINDOW, :] = kext[T:T + WINDOW, :]
        vext[0:WINDOW, :] = vext[T:T + WINDOW, :]


def _inproj_attn(x, p, layer, cos_t, sin_t, cache_k, cache_v, *, pos_base, T, CS):
    NB, L, _ = x.shape
    n_out = min(L, WINDOW)
    ext_rows = T - CS + KEY_WIN
    G = max(g for g in range(1, NB + 1) if NB % g == 0 and g * T <= max(ROWS_ATTN, T))
    kern = functools.partial(_inproj_attn_kernel, pos_base, T, CS, G, layer)
    stream_block = lambda rows, width: pl.BlockSpec((G, rows, width), lambda b, t: (b, t, 0))
    stream_const = lambda rows, width: pl.BlockSpec((G, rows, width), lambda b, t: (b, 0, 0))
    return pl.pallas_call(
        kern,
        grid=(NB // G, L // T),
        in_specs=[
            stream_block(T, D_MODEL),
            _layer_spec(p['g1'], layer),
            _layer_spec(p['w_in'], layer),
            pl.BlockSpec((T, LANES), lambda b, t: (t, 0)),
            pl.BlockSpec((T, LANES), lambda b, t: (t, 0)),
            stream_const(WINDOW, D_KV),
            stream_const(WINDOW, D_KV),
            pl.BlockSpec(memory_space=pltpu.SMEM),
        ],
        out_specs=[
            stream_block(T, D_AB),
            stream_block(T, D_ATT),
            stream_const(n_out, D_KV),
            stream_const(n_out, D_KV),
        ],
        out_shape=[
            jax.ShapeDtypeStruct((NB, L, D_AB), F32),
            jax.ShapeDtypeStruct((NB, L, D_ATT), BF16),
            jax.ShapeDtypeStruct((NB, n_out, D_KV), F32),
            jax.ShapeDtypeStruct((NB, n_out, D_KV), F32),
        ],
        scratch_shapes=[pltpu.VMEM((G, ext_rows, D_KV), F32), pltpu.VMEM((G, ext_rows, D_KV), F32)],
        compiler_params=pltpu.CompilerParams(dimension_semantics=("arbitrary", "arbitrary"),
                                             vmem_limit_bytes=VMEM_LIMIT),
        name="inproj_attn",
    )(x, p['g1'], p['w_in'], cos_t, sin_t, cache_k, cache_v, p['sinks'])


def _lru_s5_kernel(NB, T, TS, pab_ref, cbuf0_ref, h0_ref, sr0_ref, si0_ref,
                   convw_ref, convb_ref, wgate_ref, bgate_ref, lamc_ref,
                   abar_ref, bb_ref, cc_ref, d_ref, wglu_ref, bglu_ref,
                   oab_ref, cbuf_out, h_out, sr_out, si_out,
                   uext, gu_s, a_s, b_s, bu_s, h_st, s_st):
    tc = pl.program_id(0)
    R = T * NB
    RS = TS * NB
    TAIL = (CONV_W - 1) * NB

    @pl.when(tc == 0)
    def _():
        uext[0:TAIL, :] = cbuf0_ref[...]
        h_st[...] = h0_ref[...]
        s_st[:, 0:D_STATE] = sr0_ref[...]
        s_st[:, D_STATE:] = si0_ref[...]

    def to_time_major(t0, c0):
        return jnp.swapaxes(pab_ref[:, t0:t0 + TS, c0:c0 + D_LRU], 0, 1).reshape(RS, D_LRU)

    for sb in range(T // TS):
        r0 = sb * RS
        gu_s[r0:r0 + RS, D_LRU:] = to_time_major(sb * TS, 2 * D_LRU)
        bu_s[r0:r0 + RS, :] = jnp.dot(gu_s[r0:r0 + RS, D_LRU:].astype(BF16), bb_ref[...],
                                      preferred_element_type=F32)

    for ch in range(D_STATE // SCAN_LANES):
        c0 = ch * SCAN_LANES
        ar = jnp.broadcast_to(abar_ref[0:1, c0:c0 + SCAN_LANES], (NB, SCAN_LANES))
        ai = jnp.broadcast_to(abar_ref[1:2, c0:c0 + SCAN_LANES], (NB, SCAN_LANES))

        def s5_step(t, carry, c0=c0, ar=ar, ai=ai):
            sr, si = carry
            r0 = pl.multiple_of(t * NB, NB)
            nr = ar * sr - ai * si + bu_s[pl.ds(r0, NB), c0:c0 + SCAN_LANES]
            ni = ar * si + ai * sr + bu_s[pl.ds(r0, NB), D_STATE + c0:D_STATE + c0 + SCAN_LANES]
            bu_s[pl.ds(r0, NB), c0:c0 + SCAN_LANES] = nr
            bu_s[pl.ds(r0, NB), D_STATE + c0:D_STATE + c0 + SCAN_LANES] = ni
            return nr, ni

        sr, si = lax.fori_loop(0, T, s5_step,
                               (s_st[:, c0:c0 + SCAN_LANES], s_st[:, D_STATE + c0:D_STATE + c0 + SCAN_LANES]),
                               unroll=True)
        s_st[:, c0:c0 + SCAN_LANES] = sr
        s_st[:, D_STATE + c0:D_STATE + c0 + SCAN_LANES] = si

    for sb in range(T // TS):
        r0 = sb * RS
        uext[TAIL + r0:TAIL + r0 + RS, :] = to_time_major(sb * TS, 0)
        gu_s[r0:r0 + RS, 0:D_LRU] = to_time_major(sb * TS, D_LRU)
    for sb in range(T // TS):
        r0 = sb * RS
        xc = convb_ref[...]
        for j in range(CONV_W):
            xc = xc + uext[j * NB + r0:j * NB + r0 + RS, :] * convw_ref[j:j + 1, :]
        g = jnp.dot(xc.astype(BF16), wgate_ref[...], preferred_element_type=F32) + bgate_ref[...]
        r_gate = _sigmoid(g[:, :D_LRU])
        i_gate = _sigmoid(g[:, D_LRU:])
        log_a = r_gate * lamc_ref[...]
        a_s[r0:r0 + RS, :] = jnp.exp(log_a)
        b_s[r0:r0 + RS, :] = jnp.sqrt(1.0 - jnp.exp(2.0 * log_a)) * (i_gate * xc)

    tail = uext[R:R + TAIL, :]
    uext[0:TAIL, :] = tail
    cbuf_out[...] = tail

    def lru_step(t, h):
        r0 = pl.multiple_of(t * NB, NB)
        h = a_s[pl.ds(r0, NB), :] * h + b_s[pl.ds(r0, NB), :]
        b_s[pl.ds(r0, NB), :] = h
        return h

    h_last = lax.fori_loop(0, T, lru_step, h_st[...], unroll=True)
    h_st[...] = h_last
    h_out[...] = h_last

    sr_out[...] = s_st[:, 0:D_STATE]
    si_out[...] = s_st[:, D_STATE:]

    def to_batch_major(v):
        return jnp.swapaxes(v.reshape(TS, NB, D_LRU), 0, 1).astype(BF16)

    for sb in range(T // TS):
        r0 = sb * RS
        t0 = sb * TS
        out_a = b_s[r0:r0 + RS, :] * jax.nn.gelu(gu_s[r0:r0 + RS, 0:D_LRU])
        oab_ref[:, t0:t0 + TS, 0:D_LRU] = to_batch_major(out_a)
        y = jnp.dot(bu_s[r0:r0 + RS, :].astype(BF16), cc_ref[...], preferred_element_type=F32)
        y = y + d_ref[...] * gu_s[r0:r0 + RS, D_LRU:]
        z = jax.nn.gelu(y)
        gl = jnp.dot(z.astype(BF16), wglu_ref[...], preferred_element_type=F32) + bglu_ref[...]
        oab_ref[:, t0:t0 + TS, D_LRU:] = to_batch_major(z * _sigmoid(gl))


def _lru_s5(pab, cbuf0, h0, sr0, si0, p, layer, *, T):
    NB, L, _ = pab.shape
    R = T * NB
    TAIL = (CONV_W - 1) * NB
    TS = min(T, SCAN_SUB)
    kern = functools.partial(_lru_s5_kernel, NB, T, TS)
    states = [cbuf0, h0, sr0, si0]
    params = [p[k] for k in ('convw', 'convb', 'wgate', 'bgate', 'lamc', 'abar', 'bb', 'cc', 'd', 'wglu', 'bglu')]
    return pl.pallas_call(
        kern,
        grid=(L // T,),
        in_specs=([pl.BlockSpec((NB, T, D_AB), lambda t: (0, t, 0))] + [_const_spec(c.shape) for c in states]
                  + [_layer_spec(c, layer) for c in params]),
        out_specs=[
            pl.BlockSpec((NB, T, 2 * D_LRU), lambda t: (0, t, 0)),
            pl.BlockSpec((TAIL, D_LRU), lambda t: (0, 0)),
            pl.BlockSpec((NB, D_LRU), lambda t: (0, 0)),
            pl.BlockSpec((NB, D_STATE), lambda t: (0, 0)),
            pl.BlockSpec((NB, D_STATE), lambda t: (0, 0)),
        ],
        out_shape=[
            jax.ShapeDtypeStruct((NB, L, 2 * D_LRU), BF16),
            jax.ShapeDtypeStruct((TAIL, D_LRU), F32),
            jax.ShapeDtypeStruct((NB, D_LRU), F32),
            jax.ShapeDtypeStruct((NB, D_STATE), F32),
            jax.ShapeDtypeStruct((NB, D_STATE), F32),
        ],
        scratch_shapes=[
            pltpu.VMEM((R + TAIL, D_LRU), F32),
            pltpu.VMEM((R, 2 * D_LRU), F32),
            pltpu.VMEM((R, D_LRU), F32),
            pltpu.VMEM((R, D_LRU), F32),
            pltpu.VMEM((R, 2 * D_STATE), F32),
            pltpu.VMEM((NB, D_LRU), F32),
            pltpu.VMEM((NB, 2 * D_STATE), F32),
        ],
        compiler_params=pltpu.CompilerParams(dimension_semantics=("arbitrary",), vmem_limit_bytes=VMEM_LIMIT),
        name="lru_s5",
    )(pab, *states, *params)


def _outproj_mlp_kernel(final, x_ref, oab_ref, oc_ref, wout_ref, g2_ref, wup_ref, wdn_ref, gf_ref, y_ref):
    x = x_ref[0]
    mix = jnp.dot(oab_ref[0], wout_ref[0:2 * D_LRU, :], preferred_element_type=F32)
    mix = mix + jnp.dot(oc_ref[0], wout_ref[2 * D_LRU:, :], preferred_element_type=F32)
    x = x + mix
    hm = _rms(x, g2_ref[...]).astype(BF16)
    down = None
    for c in range(D_FF // FF_CHUNK):
        up = jnp.dot(hm, wup_ref[:, c * FF_CHUNK:(c + 1) * FF_CHUNK], preferred_element_type=F32)
        act = jnp.square(jnp.maximum(up, 0.0)).astype(BF16)
        part = jnp.dot(act, wdn_ref[c * FF_CHUNK:(c + 1) * FF_CHUNK, :], preferred_element_type=F32)
        down = part if down is None else down + part
    x = x + down
    if final:
        x = _rms(x, gf_ref[...])
    y_ref[0] = x


def _outproj_mlp(x, oab, oc, p, layer, gf, *, final, T):
    NB, L, _ = x.shape
    kern = functools.partial(_outproj_mlp_kernel, final)
    row_block = lambda width: pl.BlockSpec((1, T, width), lambda b, t: (b, t, 0))
    return pl.pallas_call(
        kern,
        grid=(NB, L // T),
        in_specs=[
            row_block(D_MODEL), row_block(2 * D_LRU), row_block(D_ATT),
            _layer_spec(p['w_out'], layer), _layer_spec(p['g2'], layer),
            _layer_spec(p['w_up'], layer), _layer_spec(p['w_down'], layer),
            _const_spec((1, D_MODEL)),
        ],
        out_specs=row_block(D_MODEL),
        out_shape=jax.ShapeDtypeStruct((NB, L, D_MODEL), F32),
        compiler_params=pltpu.CompilerParams(dimension_semantics=("arbitrary", "arbitrary"),
                                             vmem_limit_bytes=VMEM_LIMIT_MLP),
        name="outproj_mlp",
    )(x, oab, oc, p['w_out'], p['g2'], p['w_up'], p['w_down'], gf)


def _block_diag(blocks):
    n, r, c = blocks.shape[-3:]
    on_diag = jnp.eye(n, dtype=bool)[:, None, :, None]
    full = jnp.where(on_diag, blocks[..., :, :, None, :], jnp.zeros((), blocks.dtype))
    return full.reshape(blocks.shape[:-3] + (n * r, n * c))


def _rope_tables(pos_base, L):
    half = HEAD_DIM // 2
    inv = ROPE_THETA ** (-jnp.arange(half, dtype=F32) / half)
    ang = (pos_base + jnp.arange(L, dtype=jnp.int32)).astype(F32)[:, None] * inv[None, :]
    cos, sin = jnp.cos(ang), jnp.sin(ang)
    reps = LANES // HEAD_DIM
    return (jnp.tile(jnp.concatenate([cos, cos], axis=-1), (1, reps)),
            jnp.tile(jnp.concatenate([-sin, sin], axis=-1), (1, reps)))


def _prepare_params(norm1, w_in, conv_w, conv_b, w_rg, b_rg, w_ig, b_ig, lru_lambda,
                    ssm_a_re, ssm_a_im, ssm_b_re, ssm_b_im, ssm_c_re, ssm_c_im, ssm_d, ssm_log_dt,
                    w_glu, b_glu, attn_sinks, w_out, norm2, w_up, w_down):
    depth = w_in.shape[0]
    dt = jnp.exp(ssm_log_dt)[..., None]
    lr, li = ssm_a_re, ssm_a_im
    mag = jnp.exp(lr * dt)
    abar_r, abar_i = mag * jnp.cos(li * dt), mag * jnp.sin(li * dt)
    den = lr * lr + li * li
    nr = abar_r - 1.0
    fr = (nr * lr + abar_i * li) / den
    fi = (abar_i * lr - nr * li) / den
    bb_r = fr[..., None] * ssm_b_re - fi[..., None] * ssm_b_im
    bb_i = fr[..., None] * ssm_b_im + fi[..., None] * ssm_b_re
    bb = jnp.concatenate([_block_diag(jnp.swapaxes(bb_r, -1, -2)), _block_diag(jnp.swapaxes(bb_i, -1, -2))], axis=-1)
    cc = jnp.concatenate([_block_diag(jnp.swapaxes(ssm_c_re, -1, -2)),
                          -_block_diag(jnp.swapaxes(ssm_c_im, -1, -2))], axis=-2)
    row = lambda a: a[:, None, :]
    return dict(
        g1=row(norm1), w_in=w_in.astype(BF16),
        convw=conv_w, convb=row(conv_b),
        wgate=jnp.concatenate([_block_diag(w_rg), _block_diag(w_ig)], axis=-1).astype(BF16),
        bgate=row(jnp.concatenate([b_rg, b_ig], axis=-1)),
        lamc=row(-LRU_C * jax.nn.softplus(-lru_lambda)),
        abar=jnp.stack([abar_r.reshape(depth, D_STATE), abar_i.reshape(depth, D_STATE)], axis=1),
        bb=bb.astype(BF16), cc=cc.astype(BF16), d=row(ssm_d),
        wglu=w_glu.astype(BF16), bglu=row(b_glu),
        sinks=attn_sinks, w_out=w_out.astype(BF16), g2=row(norm2),
        w_up=w_up.astype(BF16), w_down=w_down.astype(BF16),
    )


def _layer(x, p, layer, tables, conv_buf, h0, s_re, s_im, cache_k, cache_v, gf, *, pos_base, final, CS):
    NB, L, _ = x.shape
    pab, oc, k_new, v_new = _inproj_attn(x, p, layer, tables[0], tables[1], cache_k, cache_v,
                                         pos_base=pos_base, T=min(T_ATTN, L), CS=CS)
    cbuf0 = jnp.swapaxes(conv_buf, 0, 1).reshape((CONV_W - 1) * NB, D_LRU)
    oab, cbuf, h_last, sr, si = _lru_s5(
        pab, cbuf0, h0, s_re.reshape(NB, D_STATE), s_im.reshape(NB, D_STATE), p, layer, T=min(T_SCAN, L))
    if NB * L <= T_MLP:
        fold = lambda a: a.reshape(1, NB * L, a.shape[-1])
        x_new = _outproj_mlp(fold(x), fold(oab), fold(oc), p, layer, gf,
                             final=final, T=NB * L).reshape(NB, L, D_MODEL)
    else:
        x_new = _outproj_mlp(x, oab, oc, p, layer, gf, final=final, T=min(T_MLP, L))
    n_out = k_new.shape[1]
    state = (jnp.swapaxes(cbuf.reshape(CONV_W - 1, NB, D_LRU), 0, 1), h_last,
             sr.reshape(NB, SSM_GROUPS, SSM_STATE), si.reshape(NB, SSM_GROUPS, SSM_STATE),
             k_new.reshape(NB, n_out, N_KV_HEADS, HEAD_DIM), v_new.reshape(NB, n_out, N_KV_HEADS, HEAD_DIM))
    return x_new, state


def kernel(x_prompt, x_sample, cache_conv_a, state_lru, state_ssm_re, state_ssm_im, cache_k, cache_v, norm1, w_in, conv_w, conv_b, w_rg, b_rg, w_ig, b_ig, lru_lambda, ssm_a_re, ssm_a_im, ssm_b_re, ssm_b_im, ssm_c_re, ssm_c_im, ssm_d, ssm_log_dt, w_glu, b_glu, attn_sinks, w_out, norm2, w_up, w_down, norm_f):
    Bp, Lp = x_prompt.shape[0], x_prompt.shape[1]
    Bs, Ls = x_sample.shape[0], x_sample.shape[1]
    depth = w_in.shape[0]
    tab_p = _rope_tables(0, Lp)
    tab_s = _rope_tables(PAST_LEN, Ls)
    gf = norm_f[None, :]
    zconv = jnp.zeros((Bp, CONV_W - 1, D_LRU), F32)
    zh = jnp.zeros((Bp, D_LRU), F32)
    zs = jnp.zeros((Bp, SSM_GROUPS, SSM_STATE), F32)
    zkv = jnp.zeros((Bp, WINDOW, D_KV), F32)

    p = _prepare_params(norm1, w_in, conv_w, conv_b, w_rg, b_rg, w_ig, b_ig, lru_lambda,
                        ssm_a_re, ssm_a_im, ssm_b_re, ssm_b_im, ssm_c_re, ssm_c_im, ssm_d, ssm_log_dt,
                        w_glu, b_glu, attn_sinks, w_out, norm2, w_up, w_down)
    xp, xs = x_prompt, x_sample
    st_p, st_s = [], []
    for l in range(depth):
        final = l == depth - 1
        xp, sp = _layer(xp, p, l, tab_p, zconv, zh, zs, zs, zkv, zkv, gf, pos_base=0, final=final, CS=CHUNK)
        xs, ss = _layer(xs, p, l, tab_s, cache_conv_a[l], state_lru[l], state_ssm_re[l], state_ssm_im[l],
                        cache_k[l].reshape(Bs, WINDOW, D_KV), cache_v[l].reshape(Bs, WINDOW, D_KV), gf,
                        pos_base=PAST_LEN, final=final, CS=Ls)
        st_p.append(sp)
        st_s.append(ss)

    outs = [xp, xs]
    for sts in (st_p, st_s):
        for i in range(6):
            outs.append(jnp.stack([s[i] for s in sts], 0))
    return tuple(outs)
```
